```python
import math
import jax, jax.numpy as jnp
from jax import lax
import numpy as np

D_MODEL = 4096
BATCH = 1
SEQ = 8192
DEPTH = 1
DEC_BATCH = 32
DEC_SEQ = 8
PAST_LEN = 8192
PAGE_SIZE = 128

HG_WIDTH = D_MODEL // 2
HG_DK = 128
HG_DV = 128
HG_HEADS = HG_WIDTH // HG_DK
HG_CHUNK = 64
DA_DH = 128
DA_WIDTH = D_MODEL // 2
DA_HEADS = DA_WIDTH // (2 * DA_DH)
Q_BLOCK = 128
REL_BUCKETS = 32
REL_MAX_DIST = 128
EPS = 1e-6
IN_SIZES = (HG_WIDTH, HG_WIDTH, HG_HEADS * HG_DV, HG_HEADS * HG_DV,
            DA_HEADS * 2 * DA_DH, DA_HEADS * 2 * DA_DH, DA_WIDTH, DA_WIDTH,
            D_MODEL, D_MODEL)
N_IN = sum(IN_SIZES)

kernel_name = "hgrn2_diffattn_gated_hybrid_step"


def rms_norm(x, g):
    xf = x.astype(jnp.float32)
    y = xf * lax.rsqrt(jnp.mean(xf * xf, axis=-1, keepdims=True) + EPS) * g.astype(jnp.float32)
    return y.astype(x.dtype)


def split_columns(h):
    outs, start = [], 0
    for n in IN_SIZES:
        outs.append(h[..., start:start + n])
        start += n
    return outs


def t5_bucket(rel):
    n = jnp.maximum(rel, 0)
    max_exact = REL_BUCKETS // 2
    nf = jnp.maximum(n, 1).astype(jnp.float32)
    large = max_exact + (jnp.log(nf / max_exact) / math.log(REL_MAX_DIST / max_exact)
                         * (REL_BUCKETS - max_exact)).astype(jnp.int32)
    large = jnp.minimum(large, REL_BUCKETS - 1)
    return jnp.where(n < max_exact, n, large)


def hgrn2_recurrence(q, k, v, logf, s0):
    B, L, H, DK = q.shape
    DV = v.shape[-1]
    C = math.gcd(L, HG_CHUNK)
    N = L // C
    f32 = jnp.float32

    def to_chunks(t):
        return t.astype(f32).reshape(B, N, C, H, t.shape[-1]).swapaxes(0, 1)

    causal = jnp.tril(jnp.ones((C, C), dtype=bool))[None, :, :, None, None]

    def step(S, inp):
        qc, kc, vc, lc = inp
        b = jnp.cumsum(lc, axis=1)
        o_inter = jnp.einsum("bthk,bhkv->bthv", qc * jnp.exp(b), S)
        dec = jnp.exp(jnp.where(causal, b[:, :, None] - b[:, None, :], -jnp.inf))
        att = jnp.einsum("bthk,bshk,btshk->bhts", qc, kc, dec)
        o_intra = jnp.einsum("bhts,bshv->bthv", att, vc)
        b_last = b[:, -1]
        S = jnp.exp(b_last)[..., None] * S + jnp.einsum(
            "bshk,bshv->bhkv", kc * jnp.exp(b_last[:, None] - b), vc)
        return S, o_inter + o_intra

    S, o = lax.scan(step, s0.astype(f32), (to_chunks(q), to_chunks(k), to_chunks(v), to_chunks(logf)))
    return o.swapaxes(0, 1).reshape(B, L, H, DV), S


def diff_attention(q, q_pos, k, v, k_pos, lam, rel_bias):
    s = jnp.einsum("bqhcd,bkhcd->bhcqk", q, k).astype(jnp.float32) * (DA_DH ** -0.5)
    bias = rel_bias.astype(jnp.float32)[t5_bucket(q_pos[:, None] - k_pos[None, :])]
    s = s + jnp.transpose(bias, (2, 0, 1))[None, :, None]
    s = jnp.where((k_pos[None, :] <= q_pos[:, None])[None, None, None], s, -jnp.inf)
    p = jax.nn.softmax(s, axis=-1)
    a = p[:, :, 0] - lam * p[:, :, 1]
    return jnp.einsum("bhqk,bkhe->bqhe", a, v.astype(jnp.float32))


def hybrid_layer(x, s0, past_k, past_v, g_pre, g_post, w_in, lb, hg_g, lam_vecs, lam_init,
                 sub_g, rel_bias, w_up_hg, w_up_da, w_out):
    B, L, _ = x.shape
    dt = x.dtype
    f32 = jnp.float32
    xn = rms_norm(x, g_pre)
    proj = jnp.einsum("bld,dn->bln", xn, w_in)
    hq, hf, hi, hz, aq, ak, av, az, gh, ga = split_columns(proj)

    lbh = lb.reshape(HG_HEADS, HG_DK).astype(f32)
    f = lbh + (1.0 - lbh) * jax.nn.sigmoid(hf.reshape(B, L, HG_HEADS, HG_DK).astype(f32))
    o_h, s_new = hgrn2_recurrence(hq.reshape(B, L, HG_HEADS, HG_DK), 1.0 - f,
                                  hi.reshape(B, L, HG_HEADS, HG_DV), jnp.log(f), s0)
    o_h = rms_norm(o_h, hg_g).astype(dt).reshape(B, L, HG_HEADS * HG_DV) * jax.nn.silu(hz)

    q = aq.reshape(B, L, DA_HEADS, 2, DA_DH)
    k = ak.reshape(B, L, DA_HEADS, 2, DA_DH)
    v = av.reshape(B, L, DA_HEADS, 2 * DA_DH)
    lv = lam_vecs.astype(f32)
    lam = jnp.exp(jnp.sum(lv[0] * lv[1])) - jnp.exp(jnp.sum(lv[2] * lv[3])) + lam_init
    if past_k is None:
        q_pos = jnp.arange(L, dtype=jnp.int32)
        qb = math.gcd(L, Q_BLOCK)
        nb = L // qb
        q_blocks = q.reshape(B, nb, qb, DA_HEADS, 2, DA_DH).swapaxes(0, 1)
        p_blocks = q_pos.reshape(nb, qb)
        o_a = lax.map(lambda a: diff_attention(a[0], a[1], k, v, q_pos, lam, rel_bias),
                      (q_blocks, p_blocks))
        o_a = o_a.swapaxes(0, 1).reshape(B, L, DA_HEADS, 2 * DA_DH)
    else:
        past = past_k.shape[1]
        q_pos = past + jnp.arange(L, dtype=jnp.int32)
        k_pos = jnp.arange(past + L, dtype=jnp.int32)
        keys = jnp.concatenate([past_k.astype(k.dtype), k], axis=1)
        vals = jnp.concatenate([past_v.astype(v.dtype), v], axis=1)
        o_a = diff_attention(q, q_pos, keys, vals, k_pos, lam, rel_bias)
    o_a = (rms_norm(o_a, sub_g) * (1.0 - lam_init)).astype(dt).reshape(B, L, DA_WIDTH) * jax.nn.silu(az)

    merged = (jax.nn.sigmoid(gh) * jnp.einsum("blc,cd->bld", o_h, w_up_hg)
              + jax.nn.sigmoid(ga) * jnp.einsum("blc,cd->bld", o_a, w_up_da))
    out = jnp.einsum("bld,de->ble", merged, w_out)
    y = x + rms_norm(out, g_post)
    return y, k, v, s_new


def setup_inputs(seed: int = 0) -> dict:
    key = jax.random.key(seed)
    ks = jax.random.split(key, 17)
    n_pages = PAST_LEN // PAGE_SIZE
    n_used = DEC_BATCH * n_pages
    n_pool = n_used + max(1, n_used // 4)
    perm = jax.random.permutation(ks[0], n_pool)
    page_table = perm[:n_used].reshape(DEC_BATCH, n_pages).astype(jnp.int32)
    nrm = jax.random.normal
    return {
        "x_prompt": nrm(ks[1], (BATCH, SEQ, D_MODEL), jnp.float32),
        "x_sample": nrm(ks[2], (DEC_BATCH, DEC_SEQ, D_MODEL), jnp.float32),
        "cache_k": nrm(ks[3], (DEPTH, n_pool, PAGE_SIZE, DA_HEADS, 2, DA_DH), jnp.float32),
        "cache_v": nrm(ks[4], (DEPTH, n_pool, PAGE_SIZE, DA_HEADS, 2 * DA_DH), jnp.float32),
        "state_hgrn": 0.5 * nrm(ks[5], (DEPTH, DEC_BATCH, HG_HEADS, HG_DK, HG_DV), jnp.float32),
        "page_table": page_table,
        "norm_pre": 1.0 + 0.05 * nrm(ks[6], (DEPTH, D_MODEL), jnp.float32),
        "norm_post": 1.0 + 0.05 * nrm(ks[7], (DEPTH, D_MODEL), jnp.float32),
        "w_in": nrm(ks[8], (DEPTH, D_MODEL, N_IN), jnp.float32) * D_MODEL ** -0.5,
        "hg_lower": 0.1 * nrm(ks[9], (DEPTH + 1, HG_WIDTH), jnp.float32),
        "hg_norm": 1.0 + 0.05 * nrm(ks[10], (DEPTH, HG_DV), jnp.float32),
        "da_lambda": 0.1 * nrm(ks[11], (DEPTH, 4, DA_DH), jnp.float32),
        "da_subln": 1.0 + 0.05 * nrm(ks[12], (DEPTH, 2 * DA_DH), jnp.float32),
        "rel_bias": 0.5 * nrm(ks[13], (REL_BUCKETS, DA_HEADS), jnp.float32),
        "w_up_hg": nrm(ks[14], (DEPTH, HG_HEADS * HG_DV, D_MODEL), jnp.float32) * (HG_HEADS * HG_DV) ** -0.5,
        "w_up_da": nrm(ks[15], (DEPTH, DA_WIDTH, D_MODEL), jnp.float32) * DA_WIDTH ** -0.5,
        "w_out": nrm(ks[16], (DEPTH, D_MODEL, D_MODEL), jnp.float32) * D_MODEL ** -0.5,
    }


def reference(x_prompt, x_sample, cache_k, cache_v, state_hgrn, page_table, norm_pre, norm_post,
              w_in, hg_lower, hg_norm, da_lambda, da_subln, rel_bias, w_up_hg, w_up_da, w_out):
    lb_all = jnp.cumsum(jax.nn.softmax(hg_lower.astype(jnp.float32), axis=0), axis=0)[:DEPTH]
    hp, hs = x_prompt, x_sample
    nb_dec = x_sample.shape[0]
    kps, vps, sps, kss, vss, sss = [], [], [], [], [], []
    for l in range(DEPTH):
        lam_init = 0.8 - 0.6 * math.exp(-0.3 * l)
        weights = (norm_pre[l], norm_post[l], w_in[l], lb_all[l], hg_norm[l], da_lambda[l], lam_init,
                   da_subln[l], rel_bias, w_up_hg[l], w_up_da[l], w_out[l])
        s0p = jnp.zeros((hp.shape[0], HG_HEADS, HG_DK, HG_DV), jnp.float32)
        hp, kp, vp, sp = hybrid_layer(hp, s0p, None, None, *weights)
        past_k = cache_k[l][page_table].reshape(nb_dec, -1, DA_HEADS, 2, DA_DH)
        past_v = cache_v[l][page_table].reshape(nb_dec, -1, DA_HEADS, 2 * DA_DH)
        hs, ksm, vsm, ssm = hybrid_layer(hs, state_hgrn[l], past_k, past_v, *weights)
        kps.append(kp); vps.append(vp); sps.append(sp)
        kss.append(ksm); vss.append(vsm); sss.append(ssm)
    return (hp, hs, jnp.stack(kps), jnp.stack(vps), jnp.stack(sps),
            jnp.stack(kss), jnp.stack(vss), jnp.stack(sss))
```

```python
import functools
import math

import jax
import jax.numpy as jnp
from jax import lax
from jax.experimental import pallas as pl
from jax.experimental.pallas import tpu as pltpu

HEAD_DIM = 128
REL_BUCKETS = 32
REL_MAX_DIST = 128
EPS = 1e-6
NEG = -1e30
VMEM_LIMIT = 56 * 1024 * 1024
BF16 = jnp.bfloat16
F32 = jnp.float32

_NT = (((1,), (1,)), ((), ()))
_TN = (((0,), (0,)), ((), ()))


def _params(*sem):
    return pltpu.CompilerParams(dimension_semantics=sem, vmem_limit_bytes=VMEM_LIMIT)


def _silu(z):
    return z * jax.nn.sigmoid(z)


def _rmsnorm_kernel(x_ref, g_ref, o_ref):
    x = x_ref[...]
    ms = jnp.mean(x * x, axis=-1, keepdims=True)
    o_ref[...] = (x * lax.rsqrt(ms + EPS) * g_ref[...]).astype(o_ref.dtype)


def _rmsnorm_cast(x, g, tm):
    m, d = x.shape
    return pl.pallas_call(
        _rmsnorm_kernel,
        grid=(m // tm,),
        in_specs=[pl.BlockSpec((tm, d), lambda i: (i, 0)),
                  pl.BlockSpec((1, d), lambda i: (0, 0))],
        out_specs=pl.BlockSpec((tm, d), lambda i: (i, 0)),
        out_shape=jax.ShapeDtypeStruct((m, d), BF16),
        compiler_params=_params("arbitrary"),
        name="rmsnorm_cast",
    )(x, g.reshape(1, d))


def _matmul_kernel(a_ref, b_ref, o_ref):
    o_ref[...] = jnp.dot(a_ref[...], b_ref[...], preferred_element_type=F32).astype(o_ref.dtype)


def _matmul(a, b, tm, tn, out_dtype):
    m, kd = a.shape
    n = b.shape[1]
    return pl.pallas_call(
        _matmul_kernel,
        grid=(n // tn, m // tm),
        in_specs=[pl.BlockSpec((tm, kd), lambda j, i: (i, 0)),
                  pl.BlockSpec((kd, tn), lambda j, i: (0, j))],
        out_specs=pl.BlockSpec((tm, tn), lambda j, i: (i, j)),
        out_shape=jax.ShapeDtypeStruct((m, n), out_dtype),
        compiler_params=_params("arbitrary", "arbitrary"),
        name="in_proj",
    )(a, b)


def _hgrn_kernel(q_ref, f_ref, i_ref, z_ref, lo_ref, g_ref, s0_ref, o_ref, s_ref, st_ref, *,
                 chunk, layer):
    c = chunk
    step = pl.program_id(2)

    @pl.when(step == 0)
    def _():
        st_ref[...] = s0_ref[0, 0].T

    lo = lo_ref[...]
    e = jnp.exp(lo - jnp.max(lo, axis=0, keepdims=True))
    lb = jnp.sum(e[:layer + 1], axis=0, keepdims=True) / jnp.sum(e, axis=0, keepdims=True)

    q = q_ref[...]
    v = i_ref[...]
    f = lb + (1.0 - lb) * jax.nn.sigmoid(f_ref[...])
    k = 1.0 - f
    row = lax.broadcasted_iota(jnp.int32, (c, HEAD_DIM), 0)

    b = jnp.log(f)
    sh = 1
    while sh < c:
        b = b + jnp.where(row >= sh, pltpu.roll(b, sh, 0), 0.0)
        sh *= 2
    b_last = b[c - 1:c, :]

    st = st_ref[...]
    vb = v.astype(BF16)
    o = lax.dot_general((q * jnp.exp(b)).astype(BF16), st.astype(BF16), _NT,
                        preferred_element_type=F32)

    ti = lax.broadcasted_iota(jnp.int32, (c, c), 0)
    si = lax.broadcasted_iota(jnp.int32, (c, c), 1)
    att = jnp.zeros((c, c), F32)
    bend = b
    n, lg = 1, 0
    while n < c:
        bstart = pltpu.roll(bend, n, 0)
        qn = (q * jnp.exp(jnp.minimum(b - bstart, 0.0))).astype(BF16)
        kn = (k * jnp.exp(jnp.minimum(bend - b, 0.0))).astype(BF16)
        an = lax.dot_general(qn, kn, _NT, preferred_element_type=F32)
        u = ti >> lg
        w = si >> lg
        att = jnp.where(((u ^ w) * 2 + (u & 1)) == 3, an, att)
        bend = jnp.where((row & n) != 0, bend, pltpu.roll(bend, c - n, 0))
        n *= 2
        lg += 1
    diag = jnp.sum(q * k, axis=-1, keepdims=True)
    o = o + jnp.dot(att.astype(BF16), vb, preferred_element_type=F32) + diag * v

    ke = (k * jnp.exp(b_last - b)).astype(BF16)
    st_new = jnp.exp(b_last) * st + lax.dot_general(vb, ke, _TN, preferred_element_type=F32)
    st_ref[...] = st_new

    on = o * lax.rsqrt(jnp.mean(o * o, axis=-1, keepdims=True) + EPS) * g_ref[...]
    o_ref[...] = (on * _silu(z_ref[...])).astype(o_ref.dtype)

    @pl.when(step == pl.num_programs(2) - 1)
    def _():
        s_ref[0, 0] = st_new.T


def _hgrn(proj, hg_lower, hg_norm, s0, *, batch, seq, chunk, heads, col_q, layer, out_dtype):
    nc = seq // chunk
    w = heads * HEAD_DIM
    cq, cf, ci, cz = (col_q // HEAD_DIM + t * heads for t in range(4))

    def col(off):
        return pl.BlockSpec((chunk, HEAD_DIM), lambda b, h, n: (b * nc + n, off + h))

    return pl.pallas_call(
        functools.partial(_hgrn_kernel, chunk=chunk, layer=layer),
        grid=(batch, heads, nc),
        in_specs=[col(cq), col(cf), col(ci), col(cz),
                  pl.BlockSpec((hg_lower.shape[0], HEAD_DIM), lambda b, h, n: (0, h)),
                  pl.BlockSpec((1, HEAD_DIM), lambda b, h, n: (0, 0)),
                  pl.BlockSpec((1, 1, HEAD_DIM, HEAD_DIM), lambda b, h, n: (b, h, 0, 0))],
        out_specs=[pl.BlockSpec((chunk, HEAD_DIM), lambda b, h, n: (b * nc + n, h)),
                   pl.BlockSpec((1, 1, HEAD_DIM, HEAD_DIM), lambda b, h, n: (b, h, 0, 0))],
        out_shape=[jax.ShapeDtypeStruct((batch * seq, w), out_dtype),
                   jax.ShapeDtypeStruct((batch, heads, HEAD_DIM, HEAD_DIM), F32)],
        scratch_shapes=[pltpu.VMEM((HEAD_DIM, HEAD_DIM), F32)],
        compiler_params=_params("arbitrary", "arbitrary", "arbitrary"),
        name="hgrn2",
    )(proj, proj, proj, proj, hg_lower, hg_norm.reshape(1, HEAD_DIM), s0)


def _rel_bias_tile(rel, rb_ref, h):
    max_exact = REL_BUCKETS // 2
    n = jnp.maximum(rel, 0)
    nf = jnp.maximum(n, 1).astype(F32)
    large = max_exact + (jnp.log(nf / max_exact) / math.log(REL_MAX_DIST / max_exact)
                         * (REL_BUCKETS - max_exact)).astype(jnp.int32)
    bucket = jnp.where(n < max_exact, n, jnp.minimum(large, REL_BUCKETS - 1))
    far = rb_ref[REL_BUCKETS - 1, h]
    tile = jnp.zeros(rel.shape, F32)
    for bk in range(REL_BUCKETS - 1):
        tile = jnp.where(bucket == bk, rb_ref[bk, h] - far, tile)
    return tile


def _lambda(lam_ref, lam_init):
    lv = lam_ref[...]
    return (jnp.exp(jnp.sum(lv[0:1] * lv[1:2], axis=-1, keepdims=True))
            - jnp.exp(jnp.sum(lv[2:3] * lv[3:4], axis=-1, keepdims=True)) + lam_init)


def _softmax_step(s, vb, m_ref, l_ref, acc_ref, idx):
    m_prev = m_ref[idx]
    m_new = jnp.maximum(m_prev, jnp.max(s, axis=-1, keepdims=True))
    alpha = jnp.exp(m_prev - m_new)
    p = jnp.exp(s - m_new)
    l_ref[idx] = alpha * l_ref[idx] + jnp.sum(p, axis=-1, keepdims=True)
    acc_ref[idx] = alpha * acc_ref[idx] + jnp.dot(p.astype(BF16), vb, preferred_element_type=F32)
    m_ref[idx] = m_new


def _diff_head_out(acc_ref, l_ref, idx0, lam, g, z, lam_init):
    o = acc_ref[idx0] / l_ref[idx0] - lam * (acc_ref[idx0 + 1] / l_ref[idx0 + 1])
    on = o * lax.rsqrt(jnp.mean(o * o, axis=-1, keepdims=True) + EPS) * g * (1.0 - lam_init)
    return on * _silu(z)


def _attn_prompt_kernel(rb_ref, q_ref, k_ref, v_ref, z_ref, lam_ref, g_ref, o_ref,
                        m_ref, l_ref, acc_ref, bias_ref, *, tile, lam_init):
    t = tile
    h, i, j = pl.program_id(0), pl.program_id(1), pl.program_id(2)
    scale = HEAD_DIM ** -0.5

    @pl.when((i == 0) & (j == 0))
    def _():
        ti = lax.broadcasted_iota(jnp.int32, (t, t), 0)
        si = lax.broadcasted_iota(jnp.int32, (t, t), 1)
        bias_ref[0] = jnp.where(si <= ti, _rel_bias_tile(ti - si, rb_ref, h), NEG)
        bias_ref[1] = _rel_bias_tile(ti - si + t, rb_ref, h)

    @pl.when(j == 0)
    def _():
        m_ref[...] = jnp.full(m_ref.shape, NEG, F32)
        l_ref[...] = jnp.zeros(l_ref.shape, F32)
        acc_ref[...] = jnp.zeros(acc_ref.shape, F32)

    def update(bias):
        q = q_ref[...]
        k = k_ref[...]
        vb = v_ref[...].astype(BF16)
        for c in range(2):
            sl = slice(c * HEAD_DIM, (c + 1) * HEAD_DIM)
            s = lax.dot_general(q[:, sl].astype(BF16), k[:, sl].astype(BF16), _NT,
                                preferred_element_type=F32) * scale
            if bias is not None:
                s = s + bias
            _softmax_step(s, vb, m_ref, l_ref, acc_ref, c)

    @pl.when(j < i - 1)
    def _():
        update(None)

    @pl.when(j == i - 1)
    def _():
        update(bias_ref[1])

    @pl.when(j == i)
    def _():
        update(bias_ref[0])
        lam = _lambda(lam_ref, lam_init)
        o_ref[...] = _diff_head_out(acc_ref, l_ref, 0, lam, g_ref[...], z_ref[...],
                                    lam_init).astype(o_ref.dtype)


def _attn_prompt(proj, rel_bias, da_lambda, da_subln, *, seq, heads, col_q, tile, lam_init):
    hw = 2 * HEAD_DIM
    nq = seq // tile
    cq, ck, cv, cz = (col_q // hw + t * heads for t in range(4))
    return pl.pallas_call(
        functools.partial(_attn_prompt_kernel, tile=tile, lam_init=lam_init),
        grid=(heads, nq, nq),
        in_specs=[pl.BlockSpec(memory_space=pltpu.SMEM),
                  pl.BlockSpec((tile, hw), lambda h, i, j: (i, cq + h)),
                  pl.BlockSpec((tile, hw), lambda h, i, j: (jnp.minimum(j, i), ck + h)),
                  pl.BlockSpec((tile, hw), lambda h, i, j: (jnp.minimum(j, i), cv + h)),
                  pl.BlockSpec((tile, hw), lambda h, i, j: (i, cz + h)),
                  pl.BlockSpec((4, HEAD_DIM), lambda h, i, j: (0, 0)),
                  pl.BlockSpec((1, hw), lambda h, i, j: (0, 0))],
        out_specs=pl.BlockSpec((tile, hw), lambda h, i, j: (i, h)),
        out_shape=jax.ShapeDtypeStruct((seq, heads * hw), BF16),
        scratch_shapes=[pltpu.VMEM((2, tile, 1), F32), pltpu.VMEM((2, tile, 1), F32),
                        pltpu.VMEM((2, tile, hw), F32), pltpu.VMEM((2, tile, tile), F32)],
        compiler_params=_params("arbitrary", "arbitrary", "arbitrary"),
        name="attn_prompt",
    )(rel_bias, proj, proj, proj, proj, da_lambda, da_subln.reshape(1, hw))


def _attn_decode_kernel(pt_ref, rb_ref, q_ref, kn_ref, vn_ref, z_ref, kc_ref, vc_ref, lam_ref, g_ref,
                        o_ref, qx_ref, m_ref, l_ref, acc_ref, bias_ref, nbias_ref, *,
                        heads, page, n_pages, dec, lam_init):
    del pt_ref
    b, p = pl.program_id(0), pl.program_id(1)
    hw = 2 * HEAD_DIM
    r = heads * dec
    n = heads * page
    past = n_pages * page

    def near_bias(rel):
        return jnp.concatenate([_rel_bias_tile(rel[h * dec:(h + 1) * dec], rb_ref, h)
                                for h in range(heads)], axis=0)

    @pl.when((b == 0) & (p == 0))
    def _():
        row = lax.broadcasted_iota(jnp.int32, (r, n), 0)
        col = lax.broadcasted_iota(jnp.int32, (r, n), 1)
        valid = (col % heads) == (row // dec)
        bias_ref[0] = jnp.where(valid, 0.0, NEG)
        rel = (past + row % dec) - ((n_pages - 1) * page + col // heads)
        bias_ref[1] = jnp.where(valid, near_bias(rel), NEG)
        row = lax.broadcasted_iota(jnp.int32, (r, r), 0)
        col = lax.broadcasted_iota(jnp.int32, (r, r), 1)
        rel = row % dec - col % dec
        valid = jnp.where((col // dec) == (row // dec), rel, -1) >= 0
        nbias_ref[...] = jnp.where(valid, near_bias(rel), NEG)

    @pl.when(p == 0)
    def _():
        m_ref[...] = jnp.full(m_ref.shape, NEG, F32)
        l_ref[...] = jnp.zeros(l_ref.shape, F32)
        acc_ref[...] = jnp.zeros(acc_ref.shape, F32)
        scale = HEAD_DIM ** -0.5
        for c in range(2):
            qx_ref[c] = jnp.concatenate(
                [q_ref[:, (2 * h + c) * HEAD_DIM:(2 * h + c + 1) * HEAD_DIM] for h in range(heads)],
                axis=0) * scale

    def attend(keys_of, vb, bias):
        ps, alphas = [], []
        for c in range(2):
            rows = slice(c * r, (c + 1) * r)
            s = lax.dot_general(qx_ref[c].astype(BF16), keys_of(c).astype(BF16), _NT,
                                preferred_element_type=F32) + bias
            m_prev = m_ref[rows]
            m_new = jnp.maximum(m_prev, jnp.max(s, axis=-1, keepdims=True))
            alpha = jnp.exp(m_prev - m_new)
            pr = jnp.exp(s - m_new)
            l_ref[rows] = alpha * l_ref[rows] + jnp.sum(pr, axis=-1, keepdims=True)
            m_ref[rows] = m_new
            ps.append(pr.astype(BF16))
            alphas.append(alpha)
        pv = jnp.dot(jnp.concatenate(ps, axis=0), vb, preferred_element_type=F32)
        acc_ref[...] = jnp.concatenate(alphas, axis=0) * acc_ref[...] + pv

    def page_keys(c):
        return kc_ref[pl.ds(c, n, stride=2), :]

    @pl.when(p < n_pages - 1)
    def _():
        attend(page_keys, vc_ref[...].astype(BF16), bias_ref[0])

    @pl.when(p == n_pages - 1)
    def _():
        attend(page_keys, vc_ref[...].astype(BF16), bias_ref[1])

    @pl.when(p == n_pages)
    def _():
        def new_keys(c):
            return jnp.concatenate(
                [kn_ref[:, (2 * h + c) * HEAD_DIM:(2 * h + c + 1) * HEAD_DIM] for h in range(heads)],
                axis=0)
        vnew = jnp.concatenate([vn_ref[:, h * hw:(h + 1) * hw] for h in range(heads)], axis=0)
        attend(new_keys, vnew.astype(BF16), nbias_ref[...])
        lam = _lambda(lam_ref, lam_init)
        acc = acc_ref[...] / l_ref[...]
        for h in range(heads):
            o = acc[h * dec:(h + 1) * dec] - lam * acc[r + h * dec:r + (h + 1) * dec]
            on = (o * lax.rsqrt(jnp.mean(o * o, axis=-1, keepdims=True) + EPS) * g_ref[...]
                  * (1.0 - lam_init))
            sl = slice(h * hw, (h + 1) * hw)
            o_ref[:, sl] = (on * _silu(z_ref[:, sl])).astype(o_ref.dtype)


def _attn_decode(proj, cache_k, cache_v, page_table, rel_bias, da_lambda, da_subln, *,
                 batch, dec, heads, col_q, lam_init):
    hw = 2 * HEAD_DIM
    w = heads * hw
    n_pool, page = cache_k.shape[0], cache_k.shape[1]
    n_pages = page_table.shape[1]
    assert page >= REL_MAX_DIST and dec <= REL_MAX_DIST
    kc = cache_k.reshape(n_pool, page * heads * 2, HEAD_DIM)
    vc = cache_v.reshape(n_pool, page * heads, hw)
    cq, ck, cv, cz = (col_q // w + t for t in range(4))
    r = heads * dec

    def cache_spec(rows, width):
        return pl.BlockSpec((None, rows, width),
                            lambda b, p, pt: (pt[b, jnp.minimum(p, n_pages - 1)], 0, 0))

    def row_spec(cb):
        return pl.BlockSpec((dec, w), lambda b, p, pt: (b, cb))

    grid_spec = pltpu.PrefetchScalarGridSpec(
        num_scalar_prefetch=1,
        grid=(batch, n_pages + 1),
        in_specs=[pl.BlockSpec(memory_space=pltpu.SMEM),
                  row_spec(cq), row_spec(ck), row_spec(cv), row_spec(cz),
                  cache_spec(page * heads * 2, HEAD_DIM), cache_spec(page * heads, hw),
                  pl.BlockSpec((4, HEAD_DIM), lambda b, p, pt: (0, 0)),
                  pl.BlockSpec((1, hw), lambda b, p, pt: (0, 0))],
        out_specs=pl.BlockSpec((dec, w), lambda b, p, pt: (b, 0)),
        scratch_shapes=[pltpu.VMEM((2, r, HEAD_DIM), F32),
                        pltpu.VMEM((2 * r, 1), F32), pltpu.VMEM((2 * r, 1), F32),
                        pltpu.VMEM((2 * r, hw), F32),
                        pltpu.VMEM((2, r, heads * page), F32), pltpu.VMEM((r, r), F32)],
    )
    return pl.pallas_call(
        functools.partial(_attn_decode_kernel, heads=heads, page=page, n_pages=n_pages, dec=dec,
                          lam_init=lam_init),
        grid_spec=grid_spec,
        out_shape=jax.ShapeDtypeStruct((batch * dec, w), F32),
        compiler_params=_params("arbitrary", "arbitrary"),
        name="attn_decode",
    )(page_table, rel_bias, proj, proj, proj, proj, kc, vc, da_lambda, da_subln.reshape(1, hw))


def _up_kernel(oh_ref, oa_ref, wh_ref, wa_ref, gh_ref, ga_ref, o_ref):
    uh = jnp.dot(oh_ref[...].astype(BF16), wh_ref[...], preferred_element_type=F32)
    ua = jnp.dot(oa_ref[...].astype(BF16), wa_ref[...], preferred_element_type=F32)
    o_ref[...] = (jax.nn.sigmoid(gh_ref[...]) * uh + jax.nn.sigmoid(ga_ref[...]) * ua).astype(o_ref.dtype)


def _up_merge(o_h, o_a, w_h, w_a, proj, *, col_gh, col_ga, tm, tn):
    m, kh = o_h.shape
    ka = o_a.shape[1]
    d = w_h.shape[1]
    cgh, cga = col_gh // tn, col_ga // tn
    return pl.pallas_call(
        _up_kernel,
        grid=(d // tn, m // tm),
        in_specs=[pl.BlockSpec((tm, kh), lambda j, i: (i, 0)),
                  pl.BlockSpec((tm, ka), lambda j, i: (i, 0)),
                  pl.BlockSpec((kh, tn), lambda j, i: (0, j)),
                  pl.BlockSpec((ka, tn), lambda j, i: (0, j)),
                  pl.BlockSpec((tm, tn), lambda j, i: (i, cgh + j)),
                  pl.BlockSpec((tm, tn), lambda j, i: (i, cga + j))],
        out_specs=pl.BlockSpec((tm, tn), lambda j, i: (i, j)),
        out_shape=jax.ShapeDtypeStruct((m, d), BF16),
        compiler_params=_params("arbitrary", "arbitrary"),
        name="up_merge",
    )(o_h, o_a, w_h, w_a, proj, proj)


def _out_kernel(a_ref, w_ref, x_ref, g_ref, o_ref, acc_ref):
    kk = pl.program_id(1)

    @pl.when(kk == 0)
    def _():
        acc_ref[...] = jnp.zeros(acc_ref.shape, F32)

    acc_ref[...] += jnp.dot(a_ref[...], w_ref[...], preferred_element_type=F32)

    @pl.when(kk == pl.num_programs(1) - 1)
    def _():
        out = acc_ref[...]
        ms = jnp.mean(out * out, axis=-1, keepdims=True)
        o_ref[...] = x_ref[...] + out * lax.rsqrt(ms + EPS) * g_ref[...]


def _out_proj(merged, w_out, x, g_post, *, tm, tk):
    m, d = x.shape
    kd = merged.shape[1]
    return pl.pallas_call(
        _out_kernel,
        grid=(m // tm, kd // tk),
        in_specs=[pl.BlockSpec((tm, tk), lambda i, k: (i, k)),
                  pl.BlockSpec((tk, d), lambda i, k: (k, 0)),
                  pl.BlockSpec((tm, d), lambda i, k: (i, 0)),
                  pl.BlockSpec((1, d), lambda i, k: (0, 0))],
        out_specs=pl.BlockSpec((tm, d), lambda i, k: (i, 0)),
        out_shape=jax.ShapeDtypeStruct((m, d), F32),
        scratch_shapes=[pltpu.VMEM((tm, d), F32)],
        compiler_params=_params("arbitrary", "arbitrary"),
        name="out_proj",
    )(merged, w_out, x, g_post.reshape(1, d))


def _tile(n, pref):
    t = min(n, pref)
    assert n % t == 0, (n, t)
    return t


def _layer(x, s0, paged, w, *, layer, lam_init, hg_chunk, attn_tile):
    bsz, seq, d = x.shape
    m = bsz * seq
    hg_heads = (d // 2) // HEAD_DIM
    da_heads = (d // 2) // (2 * HEAD_DIM)
    hg_w = hg_heads * HEAD_DIM
    da_w = da_heads * 2 * HEAD_DIM
    col_hq = 0
    col_aq = 4 * hg_w
    col_gh = col_aq + 4 * da_w
    col_ga = col_gh + d

    x2 = x.reshape(m, d)
    tm = _tile(m, 512)
    xn = _rmsnorm_cast(x2, w["g_pre"], _tile(m, 256))
    proj = _matmul(xn, w["w_in"], tm, _tile(w["w_in"].shape[1], 1024), F32)

    chunk = math.gcd(seq, hg_chunk)
    act_dtype = BF16 if chunk % 16 == 0 else F32
    o_h, s_new = _hgrn(proj, w["hg_lower"], w["hg_norm"], s0, batch=bsz, seq=seq, chunk=chunk,
                       heads=hg_heads, col_q=col_hq, layer=layer, out_dtype=act_dtype)

    if paged is None:
        assert bsz == 1
        o_a = _attn_prompt(proj, w["rel_bias"], w["da_lambda"], w["da_subln"], seq=seq,
                           heads=da_heads, col_q=col_aq, tile=_tile(seq, attn_tile), lam_init=lam_init)
    else:
        cache_k, cache_v, page_table = paged
        o_a = _attn_decode(proj, cache_k, cache_v, page_table, w["rel_bias"], w["da_lambda"],
                           w["da_subln"], batch=bsz, dec=seq, heads=da_heads, col_q=col_aq,
                           lam_init=lam_init)

    merged = _up_merge(o_h, o_a, w["w_up_hg"], w["w_up_da"], proj, col_gh=col_gh, col_ga=col_ga,
                       tm=tm, tn=_tile(d, 512))
    y = _out_proj(merged, w["w_out"], x2, w["g_post"], tm=_tile(m, 256), tk=_tile(d, 512))

    k_new = proj[:, col_aq + da_w:col_aq + 2 * da_w].reshape(bsz, seq, da_heads, 2, HEAD_DIM)
    v_new = proj[:, col_aq + 2 * da_w:col_aq + 3 * da_w].reshape(bsz, seq, da_heads, 2 * HEAD_DIM)
    return y.reshape(bsz, seq, d), k_new, v_new, s_new


def kernel(x_prompt, x_sample, cache_k, cache_v, state_hgrn, page_table, norm_pre, norm_post, w_in,
           hg_lower, hg_norm, da_lambda, da_subln, rel_bias, w_up_hg, w_up_da, w_out):
    depth = w_in.shape[0]
    hp, hs = x_prompt, x_sample
    outs = [[] for _ in range(6)]
    for l in range(depth):
        lam_init = 0.8 - 0.6 * math.exp(-0.3 * l)
        w = dict(g_pre=norm_pre[l], g_post=norm_post[l], w_in=w_in[l].astype(BF16),
                 hg_lower=hg_lower, hg_norm=hg_norm[l], da_lambda=da_lambda[l], da_subln=da_subln[l],
                 rel_bias=rel_bias, w_up_hg=w_up_hg[l].astype(BF16), w_up_da=w_up_da[l].astype(BF16),
                 w_out=w_out[l].astype(BF16))
        hg_heads = (hp.shape[-1] // 2) // HEAD_DIM
        s0p = jnp.zeros((hp.shape[0], hg_heads, HEAD_DIM, HEAD_DIM), F32)
        common = dict(layer=l, lam_init=lam_init, hg_chunk=64, attn_tile=512)
        hp, kp, vp, sp = _layer(hp, s0p, None, w, **common)
        hs, ks, vs, ss = _layer(hs, state_hgrn[l], (cache_k[l], cache_v[l], page_table), w, **common)
        for lst, val in zip(outs, (kp, vp, sp, ks, vs, ss)):
            lst.append(val)
    return (hp, hs) + tuple(jnp.stack(o) for o in outs)
```

```python
import functools
import math

import jax
import jax.numpy as jnp
from jax import lax
from jax.experimental import pallas as pl
from jax.experimental.pallas import tpu as pltpu

HEAD_DIM = 128
REL_BUCKETS = 32
REL_MAX_DIST = 128
EPS = 1e-6
NEG = -1e30
LOG2E = math.log2(math.e)
VMEM_LIMIT = 56 * 1024 * 1024
BF16 = jnp.bfloat16
F32 = jnp.float32

_NT = (((1,), (1,)), ((), ()))
_TN = (((0,), (0,)), ((), ()))


def _params(*sem):
    return pltpu.CompilerParams(dimension_semantics=sem, vmem_limit_bytes=VMEM_LIMIT)


def _silu(z):
    return z * jax.nn.sigmoid(z)


def _tile(n, pref):
    t = min(n, pref)
    assert n % t == 0, (n, t)
    return t


def _rmsnorm_kernel(x_ref, g_ref, o_ref):
    x = x_ref[...]
    ms = jnp.mean(x * x, axis=-1, keepdims=True)
    o_ref[...] = (x * lax.rsqrt(ms + EPS) * g_ref[...]).astype(o_ref.dtype)


def _rmsnorm_cast(x, g):
    m, d = x.shape
    tm = _tile(m, 256)
    return pl.pallas_call(
        _rmsnorm_kernel,
        grid=(m // tm,),
        in_specs=[pl.BlockSpec((tm, d), lambda i: (i, 0)),
                  pl.BlockSpec((1, d), lambda i: (0, 0))],
        out_specs=pl.BlockSpec((tm, d), lambda i: (i, 0)),
        out_shape=jax.ShapeDtypeStruct((m, d), BF16),
        compiler_params=_params("arbitrary"),
        name="rmsnorm_cast",
    )(x, g.reshape(1, d))


def _proj_kernel(a_ref, w_ref, *rest, outs):
    out_refs, wb_ref = rest[:-1], rest[-1]

    @pl.when(pl.program_id(1) == 0)
    def _():
        wb_ref[...] = w_ref[...].astype(BF16)

    acc = jnp.dot(a_ref[...], wb_ref[...], preferred_element_type=F32)
    for o_ref, (_, scale, transposed) in zip(out_refs, outs):
        val = acc if scale == 1.0 else acc * scale
        o_ref[...] = (val.T if transposed else val).astype(o_ref.dtype)


def _proj(a, w, col_start, width, outs):
    m, kd = a.shape
    tm, tn = _tile(m, 512), _tile(width, 512)
    assert col_start % tn == 0
    c0 = col_start // tn
    return pl.pallas_call(
        functools.partial(_proj_kernel, outs=tuple(outs)),
        grid=(width // tn, m // tm),
        in_specs=[pl.BlockSpec((tm, kd), lambda j, i: (i, 0)),
                  pl.BlockSpec((kd, tn), lambda j, i: (0, c0 + j))],
        out_specs=[pl.BlockSpec((tn, tm), lambda j, i: (j, i)) if tr
                   else pl.BlockSpec((tm, tn), lambda j, i: (i, j)) for _, _, tr in outs],
        out_shape=[jax.ShapeDtypeStruct((width, m) if tr else (m, width), dt) for dt, _, tr in outs],
        scratch_shapes=[pltpu.VMEM((kd, tn), BF16)],
        compiler_params=_params("arbitrary", "arbitrary"),
        name="in_proj",
    )(a, w)


def _hgrn_head(q, fr, v, z, lo, g, st, *, c, layer):
    e = jnp.exp(lo - jnp.max(lo, axis=0, keepdims=True))
    lb = jnp.sum(e[:layer + 1], axis=0, keepdims=True) / jnp.sum(e, axis=0, keepdims=True)
    f = lb + (1.0 - lb) * jax.nn.sigmoid(fr)
    k = 1.0 - f
    row = lax.broadcasted_iota(jnp.int32, (c, HEAD_DIM), 0)

    b = jnp.log(f)
    sh = 1
    while sh < c:
        b = b + jnp.where(row >= sh, pltpu.roll(b, sh, 0), 0.0)
        sh *= 2
    b_last = b[c - 1:c, :]

    vb = v.astype(BF16)
    o = lax.dot_general((q * jnp.exp(b)).astype(BF16), st.astype(BF16), _NT,
                        preferred_element_type=F32)

    ti = lax.broadcasted_iota(jnp.int32, (c, c), 0)
    si = lax.broadcasted_iota(jnp.int32, (c, c), 1)
    att = jnp.zeros((c, c), F32)
    bend = b
    n, lg = 1, 0
    while n < c:
        bstart = pltpu.roll(bend, n, 0)
        qn = (q * jnp.exp(jnp.minimum(b - bstart, 0.0))).astype(BF16)
        kn = (k * jnp.exp(jnp.minimum(bend - b, 0.0))).astype(BF16)
        an = lax.dot_general(qn, kn, _NT, preferred_element_type=F32)
        u = ti >> lg
        w = si >> lg
        att = jnp.where(((u ^ w) * 2 + (u & 1)) == 3, an, att)
        bend = jnp.where((row & n) != 0, bend, pltpu.roll(bend, c - n, 0))
        n *= 2
        lg += 1
    diag = jnp.sum(q * k, axis=-1, keepdims=True)
    o = o + jnp.dot(att.astype(BF16), vb, preferred_element_type=F32) + diag * v

    ke = (k * jnp.exp(b_last - b)).astype(BF16)
    st_new = jnp.exp(b_last) * st + lax.dot_general(vb, ke, _TN, preferred_element_type=F32)
    on = o * lax.rsqrt(jnp.mean(o * o, axis=-1, keepdims=True) + EPS) * g
    return on * _silu(z), st_new


def _hgrn_kernel(q_ref, f_ref, i_ref, z_ref, lo_ref, g_ref, s0_ref, o_ref, s_ref, st_ref, *,
                 chunk, layer, hb):
    step = pl.program_id(2)
    last = step == pl.num_programs(2) - 1
    for hh in range(hb):
        sl = slice(hh * HEAD_DIM, (hh + 1) * HEAD_DIM)

        @pl.when(step == 0)
        def _():
            st_ref[hh] = s0_ref[0, hh].T

        out, st_new = _hgrn_head(q_ref[:, sl], f_ref[:, sl], i_ref[:, sl], z_ref[:, sl],
                                 lo_ref[:, sl], g_ref[...], st_ref[hh], c=chunk, layer=layer)
        st_ref[hh] = st_new
        o_ref[:, sl] = out.astype(o_ref.dtype)

        @pl.when(last)
        def _():
            s_ref[0, hh] = st_new.T


def _hgrn(hg, hg_lower, hg_norm, s0, *, batch, seq, heads, layer, chunk, hb, out_dtype):
    nc = seq // chunk
    bw = hb * HEAD_DIM
    nhb = heads // hb

    def col(t):
        return pl.BlockSpec((chunk, bw), lambda b, h, n: (b * nc + n, t * nhb + h))

    state_spec = pl.BlockSpec((1, hb, HEAD_DIM, HEAD_DIM), lambda b, h, n: (b, h, 0, 0))
    return pl.pallas_call(
        functools.partial(_hgrn_kernel, chunk=chunk, layer=layer, hb=hb),
        grid=(batch, nhb, nc),
        in_specs=[col(0), col(1), col(2), col(3),
                  pl.BlockSpec((hg_lower.shape[0], bw), lambda b, h, n: (0, h)),
                  pl.BlockSpec((1, HEAD_DIM), lambda b, h, n: (0, 0)),
                  state_spec],
        out_specs=[pl.BlockSpec((chunk, bw), lambda b, h, n: (b * nc + n, h)), state_spec],
        out_shape=[jax.ShapeDtypeStruct((batch * seq, heads * HEAD_DIM), out_dtype),
                   jax.ShapeDtypeStruct((batch, heads, HEAD_DIM, HEAD_DIM), F32)],
        scratch_shapes=[pltpu.VMEM((hb, HEAD_DIM, HEAD_DIM), F32)],
        compiler_params=_params("arbitrary", "arbitrary", "arbitrary"),
        name="hgrn2",
    )(hg, hg, hg, hg, hg_lower, hg_norm.reshape(1, HEAD_DIM), s0)


def _rel_bias_tile(rel, rb_ref, h):
    max_exact = REL_BUCKETS // 2
    n = jnp.maximum(rel, 0)
    nf = jnp.maximum(n, 1).astype(F32)
    large = max_exact + (jnp.log(nf / max_exact) / math.log(REL_MAX_DIST / max_exact)
                         * (REL_BUCKETS - max_exact)).astype(jnp.int32)
    bucket = jnp.where(n < max_exact, n, jnp.minimum(large, REL_BUCKETS - 1))
    far = rb_ref[REL_BUCKETS - 1, h]
    tile = jnp.zeros(rel.shape, F32)
    for bk in range(REL_BUCKETS - 1):
        tile = jnp.where(bucket == bk, (rb_ref[bk, h] - far) * LOG2E, tile)
    return tile


def _lambda(lam_ref, lam_init):
    lv = lam_ref[...]
    return (jnp.exp(jnp.sum(lv[0:1] * lv[1:2], axis=-1, keepdims=True))
            - jnp.exp(jnp.sum(lv[2:3] * lv[3:4], axis=-1, keepdims=True)) + lam_init)


def _diff_out(o1, o2, lam, g, z, lam_init):
    o = o1 - lam * o2
    on = o * lax.rsqrt(jnp.mean(o * o, axis=-1, keepdims=True) + EPS) * g * (1.0 - lam_init)
    return on * _silu(z)


def _attn_prompt_kernel(rb_ref, q_ref, k_ref, vt_ref, z_ref, lam_ref, g_ref, o_ref,
                        m_ref, l_ref, acc_ref, bias_ref, *, tile, lam_init):
    t = tile
    h, i, j = pl.program_id(0), pl.program_id(1), pl.program_id(2)

    @pl.when((i == 0) & (j == 0))
    def _():
        si = lax.broadcasted_iota(jnp.int32, (t, t), 0)
        ti = lax.broadcasted_iota(jnp.int32, (t, t), 1)
        bias_ref[0] = jnp.where(si <= ti, _rel_bias_tile(ti - si, rb_ref, h), NEG)
        bias_ref[1] = _rel_bias_tile(ti - si + t, rb_ref, h)

    @pl.when(j == 0)
    def _():
        m_ref[...] = jnp.full(m_ref.shape, NEG, F32)
        l_ref[...] = jnp.zeros(l_ref.shape, F32)
        acc_ref[...] = jnp.zeros(acc_ref.shape, F32)

    def update(bias):
        vt = vt_ref[...]
        sts = []
        for c in range(2):
            sl = slice(c * HEAD_DIM, (c + 1) * HEAD_DIM)
            st = lax.dot_general(k_ref[:, sl], q_ref[:, sl], _NT, preferred_element_type=F32)
            sts.append(st if bias is None else st + bias)
        ps = []
        for c in range(2):
            m_prev = m_ref[c]
            m_new = jnp.maximum(m_prev, jnp.max(sts[c], axis=0, keepdims=True))
            alpha = jnp.exp2(m_prev - m_new)
            p = jnp.exp2(sts[c] - m_new)
            l_ref[c] = alpha * l_ref[c] + jnp.sum(p, axis=0, keepdims=True)
            m_ref[c] = m_new
            ps.append((alpha, p.astype(BF16)))
        for c in range(2):
            alpha, pb = ps[c]
            acc_ref[c] = alpha * acc_ref[c] + jnp.dot(vt, pb, preferred_element_type=F32)

    @pl.when(j < i - 1)
    def _():
        update(None)

    @pl.when(j == i - 1)
    def _():
        update(bias_ref[1])

    @pl.when(j == i)
    def _():
        update(bias_ref[0])
        lam = _lambda(lam_ref, lam_init)
        ot = acc_ref[0] / l_ref[0] - lam * (acc_ref[1] / l_ref[1])
        o = ot.T
        on = o * lax.rsqrt(jnp.mean(o * o, axis=-1, keepdims=True) + EPS) * g_ref[...] * (1.0 - lam_init)
        o_ref[...] = (on * _silu(z_ref[...])).astype(o_ref.dtype)


def _attn_prompt(aq, ak, avt, z, z_col, rel_bias, da_lambda, da_subln, *, seq, heads, lam_init):
    hw = 2 * HEAD_DIM
    tile = _tile(seq, 512)
    assert tile >= REL_MAX_DIST
    nq = seq // tile
    cz = z_col // hw
    return pl.pallas_call(
        functools.partial(_attn_prompt_kernel, tile=tile, lam_init=lam_init),
        grid=(heads, nq, nq),
        in_specs=[pl.BlockSpec(memory_space=pltpu.SMEM),
                  pl.BlockSpec((tile, hw), lambda h, i, j: (i, h)),
                  pl.BlockSpec((tile, hw), lambda h, i, j: (jnp.minimum(j, i), h)),
                  pl.BlockSpec((hw, tile), lambda h, i, j: (h, jnp.minimum(j, i))),
                  pl.BlockSpec((tile, hw), lambda h, i, j: (i, cz + h)),
                  pl.BlockSpec((4, HEAD_DIM), lambda h, i, j: (0, 0)),
                  pl.BlockSpec((1, hw), lambda h, i, j: (0, 0))],
        out_specs=pl.BlockSpec((tile, hw), lambda h, i, j: (i, h)),
        out_shape=jax.ShapeDtypeStruct((seq, heads * hw), BF16),
        scratch_shapes=[pltpu.VMEM((2, 1, tile), F32), pltpu.VMEM((2, 1, tile), F32),
                        pltpu.VMEM((2, hw, tile), F32), pltpu.VMEM((2, tile, tile), F32)],
        compiler_params=_params("arbitrary", "arbitrary", "arbitrary"),
        name="attn_prompt",
    )(rel_bias, aq, ak, avt, z, da_lambda, da_subln.reshape(1, hw))


PAGES_PER_STEP = 4


def _attn_decode_kernel(pt_ref, rb_ref, q_ref, kn_ref, vn_ref, z_ref, *rest,
                        heads, page, n_pages, dec, group, lam_init):
    del pt_ref
    kc_refs, vc_refs = rest[:group], rest[group:2 * group]
    lam_ref, g_ref, o_ref, qx_ref, m_ref, l_ref, acc_ref, bias_ref, nbias_ref = rest[2 * group:]
    b, p = pl.program_id(0), pl.program_id(1)
    hw = 2 * HEAD_DIM
    r = heads * dec
    n = heads * page
    n_groups = n_pages // group
    past = n_pages * page

    def near_bias(rel):
        return jnp.concatenate([_rel_bias_tile(rel[h * dec:(h + 1) * dec], rb_ref, h)
                                for h in range(heads)], axis=0)

    @pl.when((b == 0) & (p == 0))
    def _():
        row = lax.broadcasted_iota(jnp.int32, (r, n), 0)
        col = lax.broadcasted_iota(jnp.int32, (r, n), 1)
        valid = (col % heads) == (row // dec)
        bias_ref[0] = jnp.where(valid, 0.0, NEG)
        rel = (past + row % dec) - ((n_pages - 1) * page + col // heads)
        bias_ref[1] = jnp.where(valid, near_bias(rel), NEG)
        row = lax.broadcasted_iota(jnp.int32, (r, r), 0)
        col = lax.broadcasted_iota(jnp.int32, (r, r), 1)
        rel = row % dec - col % dec
        valid = jnp.where((col // dec) == (row // dec), rel, -1) >= 0
        nbias_ref[...] = jnp.where(valid, near_bias(rel), NEG)

    @pl.when(p == 0)
    def _():
        m_ref[...] = jnp.full(m_ref.shape, NEG, F32)
        l_ref[...] = jnp.zeros(l_ref.shape, F32)
        acc_ref[...] = jnp.zeros(acc_ref.shape, F32)
        for c in range(2):
            qx_ref[c] = jnp.concatenate(
                [q_ref[:, (2 * h + c) * HEAD_DIM:(2 * h + c + 1) * HEAD_DIM] for h in range(heads)],
                axis=0).astype(BF16)

    def attend(tiles):
        scores = [[lax.dot_general(qx_ref[c], keys_of(c).astype(BF16), _NT,
                                   preferred_element_type=F32) + bias for keys_of, _, bias in tiles]
                  for c in range(2)]
        ps, alphas = [], []
        for c in range(2):
            rows = slice(c * r, (c + 1) * r)
            m_prev = m_ref[rows]
            m_new = m_prev
            for s in scores[c]:
                m_new = jnp.maximum(m_new, jnp.max(s, axis=-1, keepdims=True))
            alpha = jnp.exp2(m_prev - m_new)
            prs = [jnp.exp2(s - m_new) for s in scores[c]]
            rowsum = jnp.sum(prs[0], axis=-1, keepdims=True)
            for pr in prs[1:]:
                rowsum = rowsum + jnp.sum(pr, axis=-1, keepdims=True)
            l_ref[rows] = alpha * l_ref[rows] + rowsum
            m_ref[rows] = m_new
            ps.append([pr.astype(BF16) for pr in prs])
            alphas.append(alpha)
        pv = None
        for ti, (_, vb, _) in enumerate(tiles):
            part = jnp.dot(jnp.concatenate([ps[0][ti], ps[1][ti]], axis=0), vb,
                           preferred_element_type=F32)
            pv = part if pv is None else pv + part
        acc_ref[...] = jnp.concatenate(alphas, axis=0) * acc_ref[...] + pv

    def page_tiles(last_bias):
        tiles = []
        for gi in range(group):
            kc_ref = kc_refs[gi]
            bias = bias_ref[last_bias if gi == group - 1 else 0]
            tiles.append((lambda c, kc_ref=kc_ref: kc_ref[pl.ds(c, n, stride=2), :],
                          vc_refs[gi][...].astype(BF16), bias))
        return tiles

    @pl.when(p < n_groups - 1)
    def _():
        attend(page_tiles(0))

    @pl.when(p == n_groups - 1)
    def _():
        attend(page_tiles(1))

    @pl.when(p == n_groups)
    def _():
        def new_keys(c):
            return jnp.concatenate(
                [kn_ref[:, (2 * h + c) * HEAD_DIM:(2 * h + c + 1) * HEAD_DIM] for h in range(heads)],
                axis=0)
        vnew = jnp.concatenate([vn_ref[:, h * hw:(h + 1) * hw] for h in range(heads)], axis=0)
        attend([(new_keys, vnew.astype(BF16), nbias_ref[...])])
        lam = _lambda(lam_ref, lam_init)
        acc = acc_ref[...] / l_ref[...]
        for h in range(heads):
            sl = slice(h * hw, (h + 1) * hw)
            o_ref[:, sl] = _diff_out(acc[h * dec:(h + 1) * dec], acc[r + h * dec:r + (h + 1) * dec],
                                     lam, g_ref[...], z_ref[:, sl], lam_init).astype(o_ref.dtype)


def _attn_decode(aq, ak, av, z, z_col, cache_k, cache_v, page_table, rel_bias, da_lambda, da_subln, *,
                 batch, dec, heads, lam_init):
    hw = 2 * HEAD_DIM
    w = heads * hw
    n_pool, page = cache_k.shape[0], cache_k.shape[1]
    n_pages = page_table.shape[1]
    group = math.gcd(n_pages, PAGES_PER_STEP)
    n_groups = n_pages // group
    assert page >= REL_MAX_DIST and dec <= REL_MAX_DIST
    kc = cache_k.reshape(n_pool, page * heads * 2, HEAD_DIM)
    vc = cache_v.reshape(n_pool, page * heads, hw)
    r = heads * dec

    def cache_spec(rows, width, gi):
        return pl.BlockSpec(
            (None, rows, width),
            lambda b, p, pt: (pt[b, jnp.minimum(p, n_groups - 1) * group + gi], 0, 0))

    def row_spec(cb):
        return pl.BlockSpec((dec, w), lambda b, p, pt: (b, cb))

    grid_spec = pltpu.PrefetchScalarGridSpec(
        num_scalar_prefetch=1,
        grid=(batch, n_groups + 1),
        in_specs=([pl.BlockSpec(memory_space=pltpu.SMEM),
                   row_spec(0), row_spec(0), row_spec(0), row_spec(z_col // w)]
                  + [cache_spec(page * heads * 2, HEAD_DIM, gi) for gi in range(group)]
                  + [cache_spec(page * heads, hw, gi) for gi in range(group)]
                  + [pl.BlockSpec((4, HEAD_DIM), lambda b, p, pt: (0, 0)),
                     pl.BlockSpec((1, hw), lambda b, p, pt: (0, 0))]),
        out_specs=pl.BlockSpec((dec, w), lambda b, p, pt: (b, 0)),
        scratch_shapes=[pltpu.VMEM((2, r, HEAD_DIM), BF16),
                        pltpu.VMEM((2 * r, 1), F32), pltpu.VMEM((2 * r, 1), F32),
                        pltpu.VMEM((2 * r, hw), F32),
                        pltpu.VMEM((2, r, heads * page), F32), pltpu.VMEM((r, r), F32)],
    )
    return pl.pallas_call(
        functools.partial(_attn_decode_kernel, heads=heads, page=page, n_pages=n_pages, dec=dec,
                          group=group, lam_init=lam_init),
        grid_spec=grid_spec,
        out_shape=jax.ShapeDtypeStruct((batch * dec, w), F32),
        compiler_params=_params("arbitrary", "arbitrary"),
        name="attn_decode",
    )(page_table, rel_bias, aq, ak, av, z, *([kc] * group), *([vc] * group),
      da_lambda, da_subln.reshape(1, hw))


def _up_kernel(oh_ref, oa_ref, wh_ref, wa_ref, gh_ref, ga_ref, o_ref, whb_ref, wab_ref):
    @pl.when(pl.program_id(1) == 0)
    def _():
        whb_ref[...] = wh_ref[...].astype(BF16)
        wab_ref[...] = wa_ref[...].astype(BF16)

    uh = jnp.dot(oh_ref[...].astype(BF16), whb_ref[...], preferred_element_type=F32)
    ua = jnp.dot(oa_ref[...].astype(BF16), wab_ref[...], preferred_element_type=F32)
    o_ref[...] = (jax.nn.sigmoid(gh_ref[...]) * uh + jax.nn.sigmoid(ga_ref[...]) * ua).astype(o_ref.dtype)


def _up_merge(o_h, o_a, w_h, w_a, gates, gh_col, ga_col):
    m, kh = o_h.shape
    ka = o_a.shape[1]
    d = w_h.shape[1]
    tm, tn = _tile(m, 512), _tile(d, 512)
    cgh, cga = gh_col // tn, ga_col // tn
    return pl.pallas_call(
        _up_kernel,
        grid=(d // tn, m // tm),
        in_specs=[pl.BlockSpec((tm, kh), lambda j, i: (i, 0)),
                  pl.BlockSpec((tm, ka), lambda j, i: (i, 0)),
                  pl.BlockSpec((kh, tn), lambda j, i: (0, j)),
                  pl.BlockSpec((ka, tn), lambda j, i: (0, j)),
                  pl.BlockSpec((tm, tn), lambda j, i: (i, cgh + j)),
                  pl.BlockSpec((tm, tn), lambda j, i: (i, cga + j))],
        out_specs=pl.BlockSpec((tm, tn), lambda j, i: (i, j)),
        out_shape=jax.ShapeDtypeStruct((m, d), BF16),
        scratch_shapes=[pltpu.VMEM((kh, tn), BF16), pltpu.VMEM((ka, tn), BF16)],
        compiler_params=_params("arbitrary", "arbitrary"),
        name="up_merge",
    )(o_h, o_a, w_h, w_a, gates, gates)


def _out_kernel(a_ref, w_ref, x_ref, g_ref, o_ref, *, rows):
    kk = pl.program_id(1)
    last = pl.num_programs(1) - 1
    for r0 in range(0, o_ref.shape[0], rows):
        rs = slice(r0, r0 + rows)
        part = jnp.dot(a_ref[rs, :], w_ref[...], preferred_element_type=F32)

        @pl.when(kk == 0)
        def _():
            o_ref[rs, :] = part

        @pl.when((kk > 0) & (kk < last))
        def _():
            o_ref[rs, :] += part

        @pl.when(kk == last)
        def _():
            out = o_ref[rs, :] + part
            ms = jnp.mean(out * out, axis=-1, keepdims=True)
            o_ref[rs, :] = x_ref[rs, :] + out * lax.rsqrt(ms + EPS) * g_ref[...]


def _out_proj(merged, w_out, x, g_post):
    m, d = x.shape
    kd = merged.shape[1]
    tm, tk = _tile(m, 512), _tile(kd, 512)
    assert kd // tk >= 2
    return pl.pallas_call(
        functools.partial(_out_kernel, rows=_tile(tm, 128)),
        grid=(m // tm, kd // tk),
        in_specs=[pl.BlockSpec((tm, tk), lambda i, k: (i, k)),
                  pl.BlockSpec((tk, d), lambda i, k: (k, 0)),
                  pl.BlockSpec((tm, d), lambda i, k: (i, 0)),
                  pl.BlockSpec((1, d), lambda i, k: (0, 0))],
        out_specs=pl.BlockSpec((tm, d), lambda i, k: (i, 0)),
        out_shape=jax.ShapeDtypeStruct((m, d), F32),
        compiler_params=_params("arbitrary", "arbitrary"),
        name="out_proj",
    )(merged, w_out, x, g_post.reshape(1, d))


def _layer(x, s0, paged, w, *, layer, lam_init):
    bsz, seq, d = x.shape
    m = bsz * seq
    hg_heads = (d // 2) // HEAD_DIM
    da_heads = (d // 2) // (2 * HEAD_DIM)
    hg_w = hg_heads * HEAD_DIM
    da_w = da_heads * 2 * HEAD_DIM
    col_aq = 4 * hg_w
    prompt = paged is None
    qscale = LOG2E * HEAD_DIM ** -0.5

    x2 = x.reshape(m, d)
    xn = _rmsnorm_cast(x2, w["g_pre"])
    w_in = w["w_in"]
    (hg,) = _proj(xn, w_in, 0, 4 * hg_w, [(F32, 1.0, False)])
    (aq,) = _proj(xn, w_in, col_aq, da_w, [(BF16 if prompt else F32, qscale, False)])
    k_outs = _proj(xn, w_in, col_aq + da_w, da_w,
                   [(F32, 1.0, False)] + ([(BF16, 1.0, False)] if prompt else []))
    v_outs = _proj(xn, w_in, col_aq + 2 * da_w, da_w,
                   [(F32, 1.0, False)] + ([(BF16, 1.0, True)] if prompt else []))
    (gates,) = _proj(xn, w_in, col_aq + 3 * da_w, da_w + 2 * d, [(F32, 1.0, False)])

    if prompt:
        chunk, hb, act_dtype = math.gcd(seq, 128), 2, BF16
    else:
        chunk, hb, act_dtype = math.gcd(seq, 64), 4, F32
    assert chunk % 16 == 0 or act_dtype == F32
    o_h, s_new = _hgrn(hg, w["hg_lower"], w["hg_norm"], s0, batch=bsz, seq=seq, heads=hg_heads,
                       layer=layer, chunk=chunk, hb=math.gcd(hg_heads, hb), out_dtype=act_dtype)

    if prompt:
        assert bsz == 1
        o_a = _attn_prompt(aq, k_outs[1], v_outs[1], gates, 0, w["rel_bias"], w["da_lambda"],
                           w["da_subln"], seq=seq, heads=da_heads, lam_init=lam_init)
    else:
        cache_k, cache_v, page_table = paged
        o_a = _attn_decode(aq, k_outs[0], v_outs[0], gates, 0, cache_k, cache_v, page_table,
                           w["rel_bias"], w["da_lambda"], w["da_subln"], batch=bsz, dec=seq,
                           heads=da_heads, lam_init=lam_init)

    merged = _up_merge(o_h, o_a, w["w_up_hg"], w["w_up_da"], gates, da_w, da_w + d)
    y = _out_proj(merged, w["w_out"], x2, w["g_post"])

    k_new = k_outs[0].reshape(bsz, seq, da_heads, 2, HEAD_DIM)
    v_new = v_outs[0].reshape(bsz, seq, da_heads, 2 * HEAD_DIM)
    return y.reshape(bsz, seq, d), k_new, v_new, s_new


def kernel(x_prompt, x_sample, cache_k, cache_v, state_hgrn, page_table, norm_pre, norm_post, w_in,
           hg_lower, hg_norm, da_lambda, da_subln, rel_bias, w_up_hg, w_up_da, w_out):
    depth = w_in.shape[0]
    hp, hs = x_prompt, x_sample
    outs = [[] for _ in range(6)]
    for l in range(depth):
        lam_init = 0.8 - 0.6 * math.exp(-0.3 * l)
        w = dict(g_pre=norm_pre[l], g_post=norm_post[l], w_in=w_in[l], hg_lower=hg_lower,
                 hg_norm=hg_norm[l], da_lambda=da_lambda[l], da_subln=da_subln[l], rel_bias=rel_bias,
                 w_up_hg=w_up_hg[l], w_up_da=w_up_da[l], w_out=w_out[l].astype(BF16))
        hg_heads = (hp.shape[-1] // 2) // HEAD_DIM
        s0p = jnp.zeros((hp.shape[0], hg_heads, HEAD_DIM, HEAD_DIM), F32)
        hp, kp, vp, sp = _layer(hp, s0p, None, w, layer=l, lam_init=lam_init)
        hs, ks, vs, ss = _layer(hs, state_hgrn[l], (cache_k[l], cache_v[l], page_table), w,
                                layer=l, lam_init=lam_init)
        for lst, val in zip(outs, (kp, vp, sp, ks, vs, ss)):
            lst.append(val)
    return (hp, hs) + tuple(jnp.stack(o) for o in outs)
```

```python
import functools
import math

import jax
import jax.numpy as jnp
from jax import lax
from jax.experimental import pallas as pl
from jax.experimental.pallas import tpu as pltpu

HEAD_DIM = 128
REL_BUCKETS = 32
REL_MAX_DIST = 128
EPS = 1e-6
NEG = -1e30
LOG2E = math.log2(math.e)
VMEM_LIMIT = 56 * 1024 * 1024
BF16 = jnp.bfloat16
F32 = jnp.float32

_NT = (((1,), (1,)), ((), ()))
_TN = (((0,), (0,)), ((), ()))


def _params(*sem):
    return pltpu.CompilerParams(dimension_semantics=sem, vmem_limit_bytes=VMEM_LIMIT)


def _silu(z):
    return z * jax.nn.sigmoid(z)


def _tile(n, pref):
    t = min(n, pref)
    assert n % t == 0, (n, t)
    return t


def _rmsnorm_kernel(x_ref, g_ref, o_ref):
    x = x_ref[...]
    ms = jnp.mean(x * x, axis=-1, keepdims=True)
    o_ref[...] = (x * lax.rsqrt(ms + EPS) * g_ref[...]).astype(o_ref.dtype)


def _rmsnorm_cast(x, g):
    m, d = x.shape
    tm = _tile(m, 256)
    return pl.pallas_call(
        _rmsnorm_kernel,
        grid=(m // tm,),
        in_specs=[pl.BlockSpec((tm, d), lambda i: (i, 0)),
                  pl.BlockSpec((1, d), lambda i: (0, 0))],
        out_specs=pl.BlockSpec((tm, d), lambda i: (i, 0)),
        out_shape=jax.ShapeDtypeStruct((m, d), BF16),
        compiler_params=_params("arbitrary"),
        name="rmsnorm_cast",
    )(x, g.reshape(1, d))


STREAMS = 4


def _row_split_specs(rows, cols, n, index):
    part = rows // n
    assert part * n == rows and part % 16 == 0

    def spec(t):
        def index_map(*grid):
            r, c = index(*grid)
            return (r * n + t, c)
        return pl.BlockSpec((part, cols), index_map)
    return [spec(t) for t in range(n)]


def _proj_kernel(*refs, ns, outs):
    a_refs, w_refs, out_refs, wb_ref = refs[:ns], refs[ns:2 * ns], refs[2 * ns:-1], refs[-1]

    @pl.when(pl.program_id(1) == 0)
    def _():
        part = w_refs[0].shape[0]
        for t, w_ref in enumerate(w_refs):
            wb_ref[t * part:(t + 1) * part, :] = w_ref[...].astype(BF16)

    a = jnp.concatenate([a_ref[...] for a_ref in a_refs], axis=0)
    acc = jnp.dot(a, wb_ref[...], preferred_element_type=F32)
    for o_ref, (_, scale, layout) in zip(out_refs, outs):
        val = acc if scale == 1.0 else acc * scale
        if layout == "tiles_t":
            o_ref[...] = val.T.astype(o_ref.dtype)
        else:
            o_ref[...] = val.astype(o_ref.dtype)


def _proj(a, w, col_start, width, outs):
    m, kd = a.shape
    tm, tn = _tile(m, 512), _tile(width, 512)
    assert col_start % tn == 0
    c0 = col_start // tn
    ns = STREAMS if tm % (16 * STREAMS) == 0 else 1

    def out_spec(layout):
        if layout == "tiles_t":
            return pl.BlockSpec((None, tn, tm), lambda j, i: (i, j, 0))
        return pl.BlockSpec((tm, tn), lambda j, i: (i, j))

    def out_shape(dt, layout):
        return jax.ShapeDtypeStruct((m // tm, width, tm) if layout == "tiles_t" else (m, width), dt)

    return pl.pallas_call(
        functools.partial(_proj_kernel, ns=ns, outs=tuple(outs)),
        grid=(width // tn, m // tm),
        in_specs=(_row_split_specs(tm, kd, ns, lambda j, i: (i, 0))
                  + _row_split_specs(kd, tn, ns, lambda j, i: (0, c0 + j))),
        out_specs=[out_spec(layout) for _, _, layout in outs],
        out_shape=[out_shape(dt, layout) for dt, _, layout in outs],
        scratch_shapes=[pltpu.VMEM((kd, tn), BF16)],
        compiler_params=_params("arbitrary", "arbitrary"),
        name="in_proj",
    )(*([a] * ns), *([w] * ns))


def _hgrn_head(q, fr, v, z, lo, g, st, *, c, layer):
    e = jnp.exp(lo - jnp.max(lo, axis=0, keepdims=True))
    lb = jnp.sum(e[:layer + 1], axis=0, keepdims=True) / jnp.sum(e, axis=0, keepdims=True)
    f = lb + (1.0 - lb) * jax.nn.sigmoid(fr)
    k = 1.0 - f
    row = lax.broadcasted_iota(jnp.int32, (c, HEAD_DIM), 0)

    b = jnp.log(f)
    sh = 1
    while sh < c:
        b = b + jnp.where(row >= sh, pltpu.roll(b, sh, 0), 0.0)
        sh *= 2
    b_last = b[c - 1:c, :]

    vb = v.astype(BF16)
    o = lax.dot_general((q * jnp.exp(b)).astype(BF16), st.astype(BF16), _NT,
                        preferred_element_type=F32)

    ti = lax.broadcasted_iota(jnp.int32, (c, c), 0)
    si = lax.broadcasted_iota(jnp.int32, (c, c), 1)
    att = jnp.zeros((c, c), F32)
    bend = b
    n, lg = 1, 0
    while n < c:
        bstart = pltpu.roll(bend, n, 0)
        qn = (q * jnp.exp(jnp.minimum(b - bstart, 0.0))).astype(BF16)
        kn = (k * jnp.exp(jnp.minimum(bend - b, 0.0))).astype(BF16)
        an = lax.dot_general(qn, kn, _NT, preferred_element_type=F32)
        u = ti >> lg
        w = si >> lg
        att = jnp.where(((u ^ w) * 2 + (u & 1)) == 3, an, att)
        bend = jnp.where((row & n) != 0, bend, pltpu.roll(bend, c - n, 0))
        n *= 2
        lg += 1
    diag = jnp.sum(q * k, axis=-1, keepdims=True)
    o = o + jnp.dot(att.astype(BF16), vb, preferred_element_type=F32) + diag * v

    ke = (k * jnp.exp(b_last - b)).astype(BF16)
    st_new = jnp.exp(b_last) * st + lax.dot_general(vb, ke, _TN, preferred_element_type=F32)
    on = o * lax.rsqrt(jnp.mean(o * o, axis=-1, keepdims=True) + EPS) * g
    return on * _silu(z), st_new


def _hgrn_kernel(q_ref, f_ref, i_ref, z_ref, lo_ref, g_ref, s0_ref, o_ref, s_ref, st_ref, *,
                 chunk, layer, hb):
    step = pl.program_id(2)
    last = step == pl.num_programs(2) - 1
    for hh in range(hb):
        sl = slice(hh * HEAD_DIM, (hh + 1) * HEAD_DIM)

        @pl.when(step == 0)
        def _():
            st_ref[hh] = s0_ref[0, hh].T

        out, st_new = _hgrn_head(q_ref[:, sl], f_ref[:, sl], i_ref[:, sl], z_ref[:, sl],
                                 lo_ref[:, sl], g_ref[...], st_ref[hh], c=chunk, layer=layer)
        st_ref[hh] = st_new
        o_ref[:, sl] = out.astype(o_ref.dtype)

        @pl.when(last)
        def _():
            s_ref[0, hh] = st_new.T


def _hgrn(hg, hg_lower, hg_norm, s0, *, batch, seq, heads, layer, chunk, hb, out_dtype):
    nc = seq // chunk
    bw = hb * HEAD_DIM
    nhb = heads // hb

    def col(t):
        return pl.BlockSpec((chunk, bw), lambda b, h, n: (b * nc + n, t * nhb + h))

    state_spec = pl.BlockSpec((1, hb, HEAD_DIM, HEAD_DIM), lambda b, h, n: (b, h, 0, 0))
    return pl.pallas_call(
        functools.partial(_hgrn_kernel, chunk=chunk, layer=layer, hb=hb),
        grid=(batch, nhb, nc),
        in_specs=[col(0), col(1), col(2), col(3),
                  pl.BlockSpec((hg_lower.shape[0], bw), lambda b, h, n: (0, h)),
                  pl.BlockSpec((1, HEAD_DIM), lambda b, h, n: (0, 0)),
                  state_spec],
        out_specs=[pl.BlockSpec((chunk, bw), lambda b, h, n: (b * nc + n, h)), state_spec],
        out_shape=[jax.ShapeDtypeStruct((batch * seq, heads * HEAD_DIM), out_dtype),
                   jax.ShapeDtypeStruct((batch, heads, HEAD_DIM, HEAD_DIM), F32)],
        scratch_shapes=[pltpu.VMEM((hb, HEAD_DIM, HEAD_DIM), F32)],
        compiler_params=_params("arbitrary", "arbitrary", "arbitrary"),
        name="hgrn2",
    )(hg, hg, hg, hg, hg_lower, hg_norm.reshape(1, HEAD_DIM), s0)


def _rel_bias_tile(rel, rb_ref, h):
    max_exact = REL_BUCKETS // 2
    n = jnp.maximum(rel, 0)
    nf = jnp.maximum(n, 1).astype(F32)
    large = max_exact + (jnp.log(nf / max_exact) / math.log(REL_MAX_DIST / max_exact)
                         * (REL_BUCKETS - max_exact)).astype(jnp.int32)
    bucket = jnp.where(n < max_exact, n, jnp.minimum(large, REL_BUCKETS - 1))
    far = rb_ref[REL_BUCKETS - 1, h]
    tile = jnp.zeros(rel.shape, F32)
    for bk in range(REL_BUCKETS - 1):
        tile = jnp.where(bucket == bk, (rb_ref[bk, h] - far) * LOG2E, tile)
    return tile


def _lambda(lam_ref, lam_init):
    lv = lam_ref[...]
    return (jnp.exp(jnp.sum(lv[0:1] * lv[1:2], axis=-1, keepdims=True))
            - jnp.exp(jnp.sum(lv[2:3] * lv[3:4], axis=-1, keepdims=True)) + lam_init)


def _diff_out(o1, o2, lam, g, z, lam_init):
    o = o1 - lam * o2
    on = o * lax.rsqrt(jnp.mean(o * o, axis=-1, keepdims=True) + EPS) * g * (1.0 - lam_init)
    return on * _silu(z)


def _attn_prompt_kernel(rb_ref, q_ref, k_ref, vt_ref, z_ref, lam_ref, g_ref, o_ref,
                        m_ref, l_ref, acc_ref, bias_ref, *, tile, lam_init):
    t = tile
    h, i = pl.program_id(0), pl.program_id(1)

    @pl.when(i == 0)
    def _():
        si = lax.broadcasted_iota(jnp.int32, (t, t), 0)
        ti = lax.broadcasted_iota(jnp.int32, (t, t), 1)
        bias_ref[0] = jnp.where(si <= ti, _rel_bias_tile(ti - si, rb_ref, h), NEG)
        bias_ref[1] = _rel_bias_tile(ti - si + t, rb_ref, h)

    m_ref[...] = jnp.full(m_ref.shape, NEG, F32)
    l_ref[...] = jnp.zeros(l_ref.shape, F32)
    acc_ref[...] = jnp.zeros(acc_ref.shape, F32)

    def update(j, bias):
        k = k_ref[j]
        vt = vt_ref[j]
        sts = []
        for c in range(2):
            sl = slice(c * HEAD_DIM, (c + 1) * HEAD_DIM)
            st = lax.dot_general(k[:, sl], q_ref[:, sl], _NT, preferred_element_type=F32)
            sts.append(st if bias is None else st + bias)
        ps = []
        for c in range(2):
            m_prev = m_ref[c]
            m_new = jnp.maximum(m_prev, jnp.max(sts[c], axis=0, keepdims=True))
            alpha = jnp.exp2(m_prev - m_new)
            p = jnp.exp2(sts[c] - m_new)
            l_ref[c] = alpha * l_ref[c] + jnp.sum(p, axis=0, keepdims=True)
            m_ref[c] = m_new
            ps.append((alpha, p.astype(BF16)))
        for c in range(2):
            alpha, pb = ps[c]
            acc_ref[c] = alpha * acc_ref[c] + jnp.dot(vt, pb, preferred_element_type=F32)

    def far_tile(j, carry):
        update(j, None)
        return carry

    lax.fori_loop(0, i - 1, far_tile, 0)

    @pl.when(i >= 1)
    def _():
        update(i - 1, bias_ref[1])

    update(i, bias_ref[0])
    lam = _lambda(lam_ref, lam_init)
    ot = acc_ref[0] / l_ref[0] - lam * (acc_ref[1] / l_ref[1])
    o = ot.T
    on = o * lax.rsqrt(jnp.mean(o * o, axis=-1, keepdims=True) + EPS) * g_ref[...] * (1.0 - lam_init)
    o_ref[...] = (on * _silu(z_ref[...])).astype(o_ref.dtype)


def _attn_prompt(aq, ak, avt, z, z_col, rel_bias, da_lambda, da_subln, *, seq, heads, lam_init):
    hw = 2 * HEAD_DIM
    nq, _, tile = avt.shape
    assert tile >= REL_MAX_DIST and nq * tile == seq
    cz = z_col // hw
    return pl.pallas_call(
        functools.partial(_attn_prompt_kernel, tile=tile, lam_init=lam_init),
        grid=(heads, nq),
        in_specs=[pl.BlockSpec(memory_space=pltpu.SMEM),
                  pl.BlockSpec((tile, hw), lambda h, i: (i, h)),
                  pl.BlockSpec((nq, tile, hw), lambda h, i: (0, 0, h)),
                  pl.BlockSpec((nq, hw, tile), lambda h, i: (0, h, 0)),
                  pl.BlockSpec((tile, hw), lambda h, i: (i, cz + h)),
                  pl.BlockSpec((4, HEAD_DIM), lambda h, i: (0, 0)),
                  pl.BlockSpec((1, hw), lambda h, i: (0, 0))],
        out_specs=pl.BlockSpec((tile, hw), lambda h, i: (i, h)),
        out_shape=jax.ShapeDtypeStruct((seq, heads * hw), BF16),
        scratch_shapes=[pltpu.VMEM((2, 1, tile), F32), pltpu.VMEM((2, 1, tile), F32),
                        pltpu.VMEM((2, hw, tile), F32), pltpu.VMEM((2, tile, tile), F32)],
        compiler_params=_params("arbitrary", "arbitrary"),
        name="attn_prompt",
    )(rel_bias, aq, ak.reshape(nq, tile, heads * hw), avt, z, da_lambda, da_subln.reshape(1, hw))


PAGES_PER_STEP = 8


def _attn_decode_kernel(pt_ref, rb_ref, q_ref, kn_ref, vn_ref, z_ref, *rest,
                        heads, page, n_pages, dec, group, lam_init):
    del pt_ref
    kc_refs, vc_refs = rest[:group], rest[group:2 * group]
    lam_ref, g_ref, o_ref, qx_ref, m_ref, l_ref, acc_ref, bias_ref, nbias_ref = rest[2 * group:]
    b, p = pl.program_id(0), pl.program_id(1)
    hw = 2 * HEAD_DIM
    r = heads * dec
    n = heads * page
    n_groups = n_pages // group
    past = n_pages * page

    def near_bias(rel):
        return jnp.concatenate([_rel_bias_tile(rel[h * dec:(h + 1) * dec], rb_ref, h)
                                for h in range(heads)], axis=0)

    @pl.when((b == 0) & (p == 0))
    def _():
        row = lax.broadcasted_iota(jnp.int32, (r, n), 0)
        col = lax.broadcasted_iota(jnp.int32, (r, n), 1)
        valid = (col % heads) == (row // dec)
        bias_ref[0] = jnp.where(valid, 0.0, NEG)
        rel = (past + row % dec) - ((n_pages - 1) * page + col // heads)
        bias_ref[1] = jnp.where(valid, near_bias(rel), NEG)
        row = lax.broadcasted_iota(jnp.int32, (r, r), 0)
        col = lax.broadcasted_iota(jnp.int32, (r, r), 1)
        rel = row % dec - col % dec
        valid = jnp.where((col // dec) == (row // dec), rel, -1) >= 0
        nbias_ref[...] = jnp.where(valid, near_bias(rel), NEG)

    @pl.when(p == 0)
    def _():
        m_ref[...] = jnp.full(m_ref.shape, NEG, F32)
        l_ref[...] = jnp.zeros(l_ref.shape, F32)
        acc_ref[...] = jnp.zeros(acc_ref.shape, F32)
        for c in range(2):
            qx_ref[c] = jnp.concatenate(
                [q_ref[:, (2 * h + c) * HEAD_DIM:(2 * h + c + 1) * HEAD_DIM] for h in range(heads)],
                axis=0).astype(BF16)

    def attend(tiles):
        scores = [[lax.dot_general(qx_ref[c], keys_of(c).astype(BF16), _NT,
                                   preferred_element_type=F32) + bias for keys_of, _, bias in tiles]
                  for c in range(2)]
        ps, alphas = [], []
        for c in range(2):
            rows = slice(c * r, (c + 1) * r)
            m_prev = m_ref[rows]
            m_new = m_prev
            for s in scores[c]:
                m_new = jnp.maximum(m_new, jnp.max(s, axis=-1, keepdims=True))
            alpha = jnp.exp2(m_prev - m_new)
            prs = [jnp.exp2(s - m_new) for s in scores[c]]
            rowsum = jnp.sum(prs[0], axis=-1, keepdims=True)
            for pr in prs[1:]:
                rowsum = rowsum + jnp.sum(pr, axis=-1, keepdims=True)
            l_ref[rows] = alpha * l_ref[rows] + rowsum
            m_ref[rows] = m_new
            ps.append([pr.astype(BF16) for pr in prs])
            alphas.append(alpha)
        pv = None
        for ti, (_, vb, _) in enumerate(tiles):
            part = jnp.dot(jnp.concatenate([ps[0][ti], ps[1][ti]], axis=0), vb,
                           preferred_element_type=F32)
            pv = part if pv is None else pv + part
        acc_ref[...] = jnp.concatenate(alphas, axis=0) * acc_ref[...] + pv

    def page_tiles(last_bias):
        tiles = []
        for gi in range(group):
            kc_ref = kc_refs[gi]
            bias = bias_ref[last_bias if gi == group - 1 else 0]
            tiles.append((lambda c, kc_ref=kc_ref: kc_ref[pl.ds(c, n, stride=2), :],
                          vc_refs[gi][...].astype(BF16), bias))
        return tiles

    @pl.when(p < n_groups - 1)
    def _():
        attend(page_tiles(0))

    @pl.when(p == n_groups - 1)
    def _():
        attend(page_tiles(1))

    @pl.when(p == n_groups)
    def _():
        def new_keys(c):
            return jnp.concatenate(
                [kn_ref[:, (2 * h + c) * HEAD_DIM:(2 * h + c + 1) * HEAD_DIM] for h in range(heads)],
                axis=0)
        vnew = jnp.concatenate([vn_ref[:, h * hw:(h + 1) * hw] for h in range(heads)], axis=0)
        attend([(new_keys, vnew.astype(BF16), nbias_ref[...])])
        lam = _lambda(lam_ref, lam_init)
        acc = acc_ref[...] / l_ref[...]
        for h in range(heads):
            sl = slice(h * hw, (h + 1) * hw)
            o_ref[:, sl] = _diff_out(acc[h * dec:(h + 1) * dec], acc[r + h * dec:r + (h + 1) * dec],
                                     lam, g_ref[...], z_ref[:, sl], lam_init).astype(o_ref.dtype)


def _attn_decode(aq, ak, av, z, z_col, cache_k, cache_v, page_table, rel_bias, da_lambda, da_subln, *,
                 batch, dec, heads, lam_init):
    hw = 2 * HEAD_DIM
    w = heads * hw
    n_pool, page = cache_k.shape[0], cache_k.shape[1]
    n_pages = page_table.shape[1]
    group = math.gcd(n_pages, PAGES_PER_STEP)
    n_groups = n_pages // group
    assert page >= REL_MAX_DIST and dec <= REL_MAX_DIST
    kc = cache_k.reshape(n_pool, page * heads * 2, HEAD_DIM)
    vc = cache_v.reshape(n_pool, page * heads, hw)
    r = heads * dec

    def cache_spec(rows, width, gi):
        return pl.BlockSpec(
            (None, rows, width),
            lambda b, p, pt: (pt[b, jnp.minimum(p, n_groups - 1) * group + gi], 0, 0))

    def row_spec(cb):
        return pl.BlockSpec((dec, w), lambda b, p, pt: (b, cb))

    grid_spec = pltpu.PrefetchScalarGridSpec(
        num_scalar_prefetch=1,
        grid=(batch, n_groups + 1),
        in_specs=([pl.BlockSpec(memory_space=pltpu.SMEM),
                   row_spec(0), row_spec(0), row_spec(0), row_spec(z_col // w)]
                  + [cache_spec(page * heads * 2, HEAD_DIM, gi) for gi in range(group)]
                  + [cache_spec(page * heads, hw, gi) for gi in range(group)]
                  + [pl.BlockSpec((4, HEAD_DIM), lambda b, p, pt: (0, 0)),
                     pl.BlockSpec((1, hw), lambda b, p, pt: (0, 0))]),
        out_specs=pl.BlockSpec((dec, w), lambda b, p, pt: (b, 0)),
        scratch_shapes=[pltpu.VMEM((2, r, HEAD_DIM), BF16),
                        pltpu.VMEM((2 * r, 1), F32), pltpu.VMEM((2 * r, 1), F32),
                        pltpu.VMEM((2 * r, hw), F32),
                        pltpu.VMEM((2, r, heads * page), F32), pltpu.VMEM((r, r), F32)],
    )
    return pl.pallas_call(
        functools.partial(_attn_decode_kernel, heads=heads, page=page, n_pages=n_pages, dec=dec,
                          group=group, lam_init=lam_init),
        grid_spec=grid_spec,
        out_shape=jax.ShapeDtypeStruct((batch * dec, w), F32),
        compiler_params=_params("arbitrary", "arbitrary"),
        name="attn_decode",
    )(page_table, rel_bias, aq, ak, av, z, *([kc] * group), *([vc] * group),
      da_lambda, da_subln.reshape(1, hw))


def _up_kernel(*refs, ns):
    oh_refs, oa_refs = refs[:ns], refs[ns:2 * ns]
    wh_ref, wa_ref, gh_ref, ga_ref, o_ref, whb_ref, wab_ref = refs[2 * ns:]

    @pl.when(pl.program_id(1) == 0)
    def _():
        whb_ref[...] = wh_ref[...].astype(BF16)
        wab_ref[...] = wa_ref[...].astype(BF16)

    oh = jnp.concatenate([r[...].astype(BF16) for r in oh_refs], axis=0)
    oa = jnp.concatenate([r[...].astype(BF16) for r in oa_refs], axis=0)
    uh = jnp.dot(oh, whb_ref[...], preferred_element_type=F32)
    ua = jnp.dot(oa, wab_ref[...], preferred_element_type=F32)
    o_ref[...] = (jax.nn.sigmoid(gh_ref[...]) * uh + jax.nn.sigmoid(ga_ref[...]) * ua).astype(o_ref.dtype)


def _up_merge(o_h, o_a, w_h, w_a, gates, gh_col, ga_col):
    m, kh = o_h.shape
    ka = o_a.shape[1]
    d = w_h.shape[1]
    tm, tn = _tile(m, 512), _tile(d, 512)
    cgh, cga = gh_col // tn, ga_col // tn
    ns = 2 if tm % 32 == 0 else 1
    return pl.pallas_call(
        functools.partial(_up_kernel, ns=ns),
        grid=(d // tn, m // tm),
        in_specs=(_row_split_specs(tm, kh, ns, lambda j, i: (i, 0))
                  + _row_split_specs(tm, ka, ns, lambda j, i: (i, 0))
                  + [pl.BlockSpec((kh, tn), lambda j, i: (0, j)),
                     pl.BlockSpec((ka, tn), lambda j, i: (0, j)),
                     pl.BlockSpec((tm, tn), lambda j, i: (i, cgh + j)),
                     pl.BlockSpec((tm, tn), lambda j, i: (i, cga + j))]),
        out_specs=pl.BlockSpec((tm, tn), lambda j, i: (i, j)),
        out_shape=jax.ShapeDtypeStruct((m, d), BF16),
        scratch_shapes=[pltpu.VMEM((kh, tn), BF16), pltpu.VMEM((ka, tn), BF16)],
        compiler_params=_params("arbitrary", "arbitrary"),
        name="up_merge",
    )(*([o_h] * ns), *([o_a] * ns), w_h, w_a, gates, gates)


def _out_kernel(a_ref, *refs, ns, rows):
    w_refs, (x_ref, g_ref, o_ref) = refs[:ns], refs[ns:]
    kk = pl.program_id(1)
    last = pl.num_programs(1) - 1
    w = jnp.concatenate([w_ref[...] for w_ref in w_refs], axis=0)
    for r0 in range(0, o_ref.shape[0], rows):
        rs = slice(r0, r0 + rows)
        part = jnp.dot(a_ref[rs, :], w, preferred_element_type=F32)

        @pl.when(kk == 0)
        def _():
            o_ref[rs, :] = part

        @pl.when((kk > 0) & (kk < last))
        def _():
            o_ref[rs, :] += part

        @pl.when(kk == last)
        def _():
            out = o_ref[rs, :] + part
            ms = jnp.mean(out * out, axis=-1, keepdims=True)
            o_ref[rs, :] = x_ref[rs, :] + out * lax.rsqrt(ms + EPS) * g_ref[...]


def _out_proj(merged, w_out, x, g_post):
    m, d = x.shape
    kd = merged.shape[1]
    tm, tk = _tile(m, 512), _tile(kd, 512)
    assert kd // tk >= 2
    ns = STREAMS if tk % (16 * STREAMS) == 0 else 1
    return pl.pallas_call(
        functools.partial(_out_kernel, ns=ns, rows=_tile(tm, 128)),
        grid=(m // tm, kd // tk),
        in_specs=([pl.BlockSpec((tm, tk), lambda i, k: (i, k))]
                  + _row_split_specs(tk, d, ns, lambda i, k: (k, 0))
                  + [pl.BlockSpec((tm, d), lambda i, k: (i, 0)),
                     pl.BlockSpec((1, d), lambda i, k: (0, 0))]),
        out_specs=pl.BlockSpec((tm, d), lambda i, k: (i, 0)),
        out_shape=jax.ShapeDtypeStruct((m, d), F32),
        compiler_params=_params("arbitrary", "arbitrary"),
        name="out_proj",
    )(merged, *([w_out] * ns), x, g_post.reshape(1, d))


def _layer(x, s0, paged, w, *, layer, lam_init):
    bsz, seq, d = x.shape
    m = bsz * seq
    hg_heads = (d // 2) // HEAD_DIM
    da_heads = (d // 2) // (2 * HEAD_DIM)
    hg_w = hg_heads * HEAD_DIM
    da_w = da_heads * 2 * HEAD_DIM
    col_aq = 4 * hg_w
    prompt = paged is None
    qscale = LOG2E * HEAD_DIM ** -0.5

    x2 = x.reshape(m, d)
    xn = _rmsnorm_cast(x2, w["g_pre"])
    w_in = w["w_in"]
    (hg,) = _proj(xn, w_in, 0, 4 * hg_w, [(F32, 1.0, "rows")])
    (aq,) = _proj(xn, w_in, col_aq, da_w, [(BF16 if prompt else F32, qscale, "rows")])
    k_outs = _proj(xn, w_in, col_aq + da_w, da_w,
                   [(F32, 1.0, "rows")] + ([(BF16, 1.0, "rows")] if prompt else []))
    v_outs = _proj(xn, w_in, col_aq + 2 * da_w, da_w,
                   [(F32, 1.0, "rows")] + ([(BF16, 1.0, "tiles_t")] if prompt else []))
    (gates,) = _proj(xn, w_in, col_aq + 3 * da_w, da_w + 2 * d, [(F32, 1.0, "rows")])

    if prompt:
        chunk, hb, act_dtype = math.gcd(seq, 128), 2, BF16
    else:
        chunk, hb, act_dtype = math.gcd(seq, 64), 4, F32
    assert chunk % 16 == 0 or act_dtype == F32
    o_h, s_new = _hgrn(hg, w["hg_lower"], w["hg_norm"], s0, batch=bsz, seq=seq, heads=hg_heads,
                       layer=layer, chunk=chunk, hb=math.gcd(hg_heads, hb), out_dtype=act_dtype)

    if prompt:
        assert bsz == 1
        o_a = _attn_prompt(aq, k_outs[1], v_outs[1], gates, 0, w["rel_bias"], w["da_lambda"],
                           w["da_subln"], seq=seq, heads=da_heads, lam_init=lam_init)
    else:
        cache_k, cache_v, page_table = paged
        o_a = _attn_decode(aq, k_outs[0], v_outs[0], gates, 0, cache_k, cache_v, page_table,
                           w["rel_bias"], w["da_lambda"], w["da_subln"], batch=bsz, dec=seq,
                           heads=da_heads, lam_init=lam_init)

    merged = _up_merge(o_h, o_a, w["w_up_hg"], w["w_up_da"], gates, da_w, da_w + d)
    y = _out_proj(merged, w["w_out"], x2, w["g_post"])

    k_new = k_outs[0].reshape(bsz, seq, da_heads, 2, HEAD_DIM)
    v_new = v_outs[0].reshape(bsz, seq, da_heads, 2 * HEAD_DIM)
    return y.reshape(bsz, seq, d), k_new, v_new, s_new


def kernel(x_prompt, x_sample, cache_k, cache_v, state_hgrn, page_table, norm_pre, norm_post, w_in,
           hg_lower, hg_norm, da_lambda, da_subln, rel_bias, w_up_hg, w_up_da, w_out):
    depth = w_in.shape[0]
    hp, hs = x_prompt, x_sample
    outs = [[] for _ in range(6)]
    for l in range(depth):
        lam_init = 0.8 - 0.6 * math.exp(-0.3 * l)
        w = dict(g_pre=norm_pre[l], g_post=norm_post[l], w_in=w_in[l], hg_lower=hg_lower,
                 hg_norm=hg_norm[l], da_lambda=da_lambda[l], da_subln=da_subln[l], rel_bias=rel_bias,
                 w_up_hg=w_up_hg[l], w_up_da=w_up_da[l], w_out=w_out[l].astype(BF16))
        hg_heads = (hp.shape[-1] // 2) // HEAD_DIM
        s0p = jnp.zeros((hp.shape[0], hg_heads, HEAD_DIM, HEAD_DIM), F32)
        hp, kp, vp, sp = _layer(hp, s0p, None, w, layer=l, lam_init=lam_init)
        hs, ks, vs, ss = _layer(hs, state_hgrn[l], (cache_k[l], cache_v[l], page_table), w,
                                layer=l, lam_init=lam_init)
        for lst, val in zip(outs, (kp, vp, sp, ks, vs, ss)):
            lst.append(val)
    return (hp, hs) + tuple(jnp.stack(o) for o in outs)
```

```python
import functools
import math

import jax
import jax.numpy as jnp
from jax import lax
from jax.experimental import pallas as pl
from jax.experimental.pallas import tpu as pltpu

HEAD_DIM = 128
REL_BUCKETS = 32
REL_MAX_DIST = 128
EPS = 1e-6
NEG = -1e30
LOG2E = math.log2(math.e)
VMEM_LIMIT = 56 * 1024 * 1024
BF16 = jnp.bfloat16
F32 = jnp.float32

_NT = (((1,), (1,)), ((), ()))
_TN = (((0,), (0,)), ((), ()))


def _params(*sem):
    return pltpu.CompilerParams(dimension_semantics=sem, vmem_limit_bytes=VMEM_LIMIT)


def _silu(z):
    return z * jax.nn.sigmoid(z)


ROW_TILE = 1024


def _tile(n, pref):
    t = min(n, pref)
    assert n % t == 0, (n, t)
    return t


def _rmsnorm_kernel(x_ref, g_ref, o_ref):
    x = x_ref[...]
    ms = jnp.mean(x * x, axis=-1, keepdims=True)
    o_ref[...] = (x * lax.rsqrt(ms + EPS) * g_ref[...]).astype(o_ref.dtype)


def _rmsnorm_cast(x, g):
    m, d = x.shape
    tm = _tile(m, 256)
    return pl.pallas_call(
        _rmsnorm_kernel,
        grid=(m // tm,),
        in_specs=[pl.BlockSpec((tm, d), lambda i: (i, 0)),
                  pl.BlockSpec((1, d), lambda i: (0, 0))],
        out_specs=pl.BlockSpec((tm, d), lambda i: (i, 0)),
        out_shape=jax.ShapeDtypeStruct((m, d), BF16),
        compiler_params=_params("arbitrary"),
        name="rmsnorm_cast",
    )(x, g.reshape(1, d))


def _proj_kernel(a_ref, w_ref, *rest, outs, cast):
    if cast:
        out_refs, wb_ref = rest[:-1], rest[-1]

        @pl.when(pl.program_id(1) == 0)
        def _():
            wb_ref[...] = w_ref[...].astype(BF16)
        wb = wb_ref[...]
    else:
        out_refs, wb = rest, w_ref[...]

    acc = jnp.dot(a_ref[...], wb, preferred_element_type=F32)
    for o_ref, (_, scale, layout) in zip(out_refs, outs):
        val = acc if scale == 1.0 else acc * scale
        o_ref[...] = (val.T if layout == "tiles_t" else val).astype(o_ref.dtype)


def _proj(a, w, col_start, width, outs, *, tn_pref, name):
    m, kd = a.shape
    tm, tn = _tile(m, ROW_TILE), _tile(width, tn_pref)
    assert col_start % tn == 0
    c0 = col_start // tn
    cast = w.dtype != BF16

    def out_spec(layout):
        if layout == "tiles_t":
            return pl.BlockSpec((None, tn, tm), lambda j, i: (i, j, 0))
        return pl.BlockSpec((tm, tn), lambda j, i: (i, j))

    def out_shape(dt, layout):
        return jax.ShapeDtypeStruct((m // tm, width, tm) if layout == "tiles_t" else (m, width), dt)

    return pl.pallas_call(
        functools.partial(_proj_kernel, outs=tuple(outs), cast=cast),
        grid=(width // tn, m // tm),
        in_specs=[pl.BlockSpec((tm, kd), lambda j, i: (i, 0)),
                  pl.BlockSpec((kd, tn), lambda j, i: (0, c0 + j))],
        out_specs=[out_spec(layout) for _, _, layout in outs],
        out_shape=[out_shape(dt, layout) for dt, _, layout in outs],
        scratch_shapes=[pltpu.VMEM((kd, tn), BF16)] if cast else [],
        compiler_params=_params("arbitrary", "arbitrary"),
        name=name,
    )(a, w)


def _hgrn_head(q, fr, v, z, lo, g, st, *, c, layer):
    e = jnp.exp(lo - jnp.max(lo, axis=0, keepdims=True))
    lb = jnp.sum(e[:layer + 1], axis=0, keepdims=True) / jnp.sum(e, axis=0, keepdims=True)
    f = lb + (1.0 - lb) * jax.nn.sigmoid(fr)
    k = 1.0 - f
    row = lax.broadcasted_iota(jnp.int32, (c, HEAD_DIM), 0)

    b = jnp.log(f)
    sh = 1
    while sh < c:
        b = b + jnp.where(row >= sh, pltpu.roll(b, sh, 0), 0.0)
        sh *= 2
    b_last = b[c - 1:c, :]

    vb = v.astype(BF16)
    o = lax.dot_general((q * jnp.exp(b)).astype(BF16), st.astype(BF16), _NT,
                        preferred_element_type=F32)

    ti = lax.broadcasted_iota(jnp.int32, (c, c), 0)
    si = lax.broadcasted_iota(jnp.int32, (c, c), 1)
    att = jnp.zeros((c, c), F32)
    bend = b
    n, lg = 1, 0
    while n < c:
        bstart = pltpu.roll(bend, n, 0)
        qn = (q * jnp.exp(jnp.minimum(b - bstart, 0.0))).astype(BF16)
        kn = (k * jnp.exp(jnp.minimum(bend - b, 0.0))).astype(BF16)
        an = lax.dot_general(qn, kn, _NT, preferred_element_type=F32)
        u = ti >> lg
        w = si >> lg
        att = jnp.where(((u ^ w) * 2 + (u & 1)) == 3, an, att)
        bend = jnp.where((row & n) != 0, bend, pltpu.roll(bend, c - n, 0))
        n *= 2
        lg += 1
    diag = jnp.sum(q * k, axis=-1, keepdims=True)
    o = o + jnp.dot(att.astype(BF16), vb, preferred_element_type=F32) + diag * v

    ke = (k * jnp.exp(b_last - b)).astype(BF16)
    st_new = jnp.exp(b_last) * st + lax.dot_general(vb, ke, _TN, preferred_element_type=F32)
    on = o * lax.rsqrt(jnp.mean(o * o, axis=-1, keepdims=True) + EPS) * g
    return on * _silu(z), st_new


def _hgrn_kernel(q_ref, f_ref, i_ref, z_ref, lo_ref, g_ref, s0_ref, o_ref, s_ref, st_ref, *,
                 chunk, layer, hb):
    step = pl.program_id(2)
    last = step == pl.num_programs(2) - 1
    for hh in range(hb):
        sl = slice(hh * HEAD_DIM, (hh + 1) * HEAD_DIM)

        @pl.when(step == 0)
        def _():
            st_ref[hh] = s0_ref[0, hh].T

        out, st_new = _hgrn_head(q_ref[:, sl], f_ref[:, sl], i_ref[:, sl], z_ref[:, sl],
                                 lo_ref[:, sl], g_ref[...], st_ref[hh], c=chunk, layer=layer)
        st_ref[hh] = st_new
        o_ref[:, sl] = out.astype(o_ref.dtype)

        @pl.when(last)
        def _():
            s_ref[0, hh] = st_new.T


def _hgrn(hg, hg_lower, hg_norm, s0, *, batch, seq, heads, layer, chunk, hb, out_dtype):
    nc = seq // chunk
    bw = hb * HEAD_DIM
    nhb = heads // hb

    def col(t):
        return pl.BlockSpec((chunk, bw), lambda b, h, n: (b * nc + n, t * nhb + h))

    state_spec = pl.BlockSpec((1, hb, HEAD_DIM, HEAD_DIM), lambda b, h, n: (b, h, 0, 0))
    return pl.pallas_call(
        functools.partial(_hgrn_kernel, chunk=chunk, layer=layer, hb=hb),
        grid=(batch, nhb, nc),
        in_specs=[col(0), col(1), col(2), col(3),
                  pl.BlockSpec((hg_lower.shape[0], bw), lambda b, h, n: (0, h)),
                  pl.BlockSpec((1, HEAD_DIM), lambda b, h, n: (0, 0)),
                  state_spec],
        out_specs=[pl.BlockSpec((chunk, bw), lambda b, h, n: (b * nc + n, h)), state_spec],
        out_shape=[jax.ShapeDtypeStruct((batch * seq, heads * HEAD_DIM), out_dtype),
                   jax.ShapeDtypeStruct((batch, heads, HEAD_DIM, HEAD_DIM), F32)],
        scratch_shapes=[pltpu.VMEM((hb, HEAD_DIM, HEAD_DIM), F32)],
        compiler_params=_params("arbitrary", "arbitrary", "arbitrary"),
        name="hgrn2",
    )(hg, hg, hg, hg, hg_lower, hg_norm.reshape(1, HEAD_DIM), s0)


def _rel_bias_tile(rel, rb_ref, h):
    max_exact = REL_BUCKETS // 2
    n = jnp.maximum(rel, 0)
    nf = jnp.maximum(n, 1).astype(F32)
    large = max_exact + (jnp.log(nf / max_exact) / math.log(REL_MAX_DIST / max_exact)
                         * (REL_BUCKETS - max_exact)).astype(jnp.int32)
    bucket = jnp.where(n < max_exact, n, jnp.minimum(large, REL_BUCKETS - 1))
    far = rb_ref[REL_BUCKETS - 1, h]
    tile = jnp.zeros(rel.shape, F32)
    for bk in range(REL_BUCKETS - 1):
        tile = jnp.where(bucket == bk, (rb_ref[bk, h] - far) * LOG2E, tile)
    return tile


def _lambda(lam_ref, lam_init):
    lv = lam_ref[...]
    return (jnp.exp(jnp.sum(lv[0:1] * lv[1:2], axis=-1, keepdims=True))
            - jnp.exp(jnp.sum(lv[2:3] * lv[3:4], axis=-1, keepdims=True)) + lam_init)


def _diff_out(o1, o2, lam, g, z, lam_init):
    o = o1 - lam * o2
    on = o * lax.rsqrt(jnp.mean(o * o, axis=-1, keepdims=True) + EPS) * g * (1.0 - lam_init)
    return on * _silu(z)


def _attn_prompt_kernel(rb_ref, q_ref, k_ref, vt_ref, z_ref, lam_ref, g_ref, o_ref,
                        m_ref, l_ref, acc_ref, bias_ref, *, tile, lam_init):
    t = tile
    h, i = pl.program_id(0), pl.program_id(1)

    @pl.when(i == 0)
    def _():
        si = lax.broadcasted_iota(jnp.int32, (t, t), 0)
        ti = lax.broadcasted_iota(jnp.int32, (t, t), 1)
        bias_ref[0] = jnp.where(si <= ti, _rel_bias_tile(ti - si, rb_ref, h), NEG)
        bias_ref[1] = _rel_bias_tile(ti - si + t, rb_ref, h)

    m_ref[...] = jnp.full(m_ref.shape, NEG, F32)
    l_ref[...] = jnp.zeros(l_ref.shape, F32)
    acc_ref[...] = jnp.zeros(acc_ref.shape, F32)

    def update(j, bias):
        k = k_ref[j]
        vt = vt_ref[j]
        sts = []
        for c in range(2):
            sl = slice(c * HEAD_DIM, (c + 1) * HEAD_DIM)
            st = lax.dot_general(k[:, sl], q_ref[:, sl], _NT, preferred_element_type=F32)
            sts.append(st if bias is None else st + bias)
        ps = []
        for c in range(2):
            m_prev = m_ref[c]
            m_new = jnp.maximum(m_prev, jnp.max(sts[c], axis=0, keepdims=True))
            alpha = jnp.exp2(m_prev - m_new)
            p = jnp.exp2(sts[c] - m_new)
            l_ref[c] = alpha * l_ref[c] + jnp.sum(p, axis=0, keepdims=True)
            m_ref[c] = m_new
            ps.append((alpha, p.astype(BF16)))
        for c in range(2):
            alpha, pb = ps[c]
            acc_ref[c] = alpha * acc_ref[c] + jnp.dot(vt, pb, preferred_element_type=F32)

    def update2(j0):
        kk = [k_ref[j0], k_ref[j0 + 1]]
        vts = [vt_ref[j0], vt_ref[j0 + 1]]
        sts = [[lax.dot_general(kk[u][:, c * HEAD_DIM:(c + 1) * HEAD_DIM],
                                q_ref[:, c * HEAD_DIM:(c + 1) * HEAD_DIM], _NT,
                                preferred_element_type=F32) for u in range(2)] for c in range(2)]
        ps = []
        for c in range(2):
            m_prev = m_ref[c]
            m_new = jnp.maximum(m_prev, jnp.maximum(jnp.max(sts[c][0], axis=0, keepdims=True),
                                                    jnp.max(sts[c][1], axis=0, keepdims=True)))
            alpha = jnp.exp2(m_prev - m_new)
            p0 = jnp.exp2(sts[c][0] - m_new)
            p1 = jnp.exp2(sts[c][1] - m_new)
            l_ref[c] = (alpha * l_ref[c] + jnp.sum(p0, axis=0, keepdims=True)
                        + jnp.sum(p1, axis=0, keepdims=True))
            m_ref[c] = m_new
            ps.append((alpha, p0.astype(BF16), p1.astype(BF16)))
        for c in range(2):
            alpha, p0, p1 = ps[c]
            acc_ref[c] = (alpha * acc_ref[c] + jnp.dot(vts[0], p0, preferred_element_type=F32)
                          + jnp.dot(vts[1], p1, preferred_element_type=F32))

    n_far = jnp.maximum(i - 1, 0)

    def far_pair(jj, carry):
        update2(2 * jj)
        return carry

    lax.fori_loop(0, n_far // 2, far_pair, 0)

    @pl.when(n_far % 2 == 1)
    def _():
        update(n_far - 1, None)

    @pl.when(i >= 1)
    def _():
        update(i - 1, bias_ref[1])

    update(i, bias_ref[0])
    lam = _lambda(lam_ref, lam_init)
    ot = acc_ref[0] / l_ref[0] - lam * (acc_ref[1] / l_ref[1])
    o = ot.T
    on = o * lax.rsqrt(jnp.mean(o * o, axis=-1, keepdims=True) + EPS) * g_ref[...] * (1.0 - lam_init)
    o_ref[...] = (on * _silu(z_ref[...])).astype(o_ref.dtype)


def _attn_prompt(aq, ak, avt, z, z_col, rel_bias, da_lambda, da_subln, *, seq, heads, lam_init):
    hw = 2 * HEAD_DIM
    nq, _, tile = avt.shape
    assert tile >= REL_MAX_DIST and nq * tile == seq
    cz = z_col // hw
    return pl.pallas_call(
        functools.partial(_attn_prompt_kernel, tile=tile, lam_init=lam_init),
        grid=(heads, nq),
        in_specs=[pl.BlockSpec(memory_space=pltpu.SMEM),
                  pl.BlockSpec((tile, hw), lambda h, i: (i, h)),
                  pl.BlockSpec((nq, tile, hw), lambda h, i: (0, 0, h)),
                  pl.BlockSpec((nq, hw, tile), lambda h, i: (0, h, 0)),
                  pl.BlockSpec((tile, hw), lambda h, i: (i, cz + h)),
                  pl.BlockSpec((4, HEAD_DIM), lambda h, i: (0, 0)),
                  pl.BlockSpec((1, hw), lambda h, i: (0, 0))],
        out_specs=pl.BlockSpec((tile, hw), lambda h, i: (i, h)),
        out_shape=jax.ShapeDtypeStruct((seq, heads * hw), BF16),
        scratch_shapes=[pltpu.VMEM((2, 1, tile), F32), pltpu.VMEM((2, 1, tile), F32),
                        pltpu.VMEM((2, hw, tile), F32), pltpu.VMEM((2, tile, tile), F32)],
        compiler_params=_params("arbitrary", "arbitrary"),
        name="attn_prompt",
    )(rel_bias, aq, ak.reshape(nq, tile, heads * hw), avt, z, da_lambda, da_subln.reshape(1, hw))


PAGES_PER_STEP = 8


def _attn_decode_kernel(pt_ref, rb_ref, q_ref, kn_ref, vn_ref, z_ref, *rest,
                        heads, page, n_pages, dec, group, lam_init):
    del pt_ref
    kc_refs, vc_refs = rest[:group], rest[group:2 * group]
    lam_ref, g_ref, o_ref, qx_ref, m_ref, l_ref, acc_ref, bias_ref, nbias_ref = rest[2 * group:]
    b, p = pl.program_id(0), pl.program_id(1)
    hw = 2 * HEAD_DIM
    r = heads * dec
    n = heads * page
    n_groups = n_pages // group
    past = n_pages * page

    def near_bias(rel):
        return jnp.concatenate([_rel_bias_tile(rel[h * dec:(h + 1) * dec], rb_ref, h)
                                for h in range(heads)], axis=0)

    @pl.when((b == 0) & (p == 0))
    def _():
        row = lax.broadcasted_iota(jnp.int32, (r, n), 0)
        col = lax.broadcasted_iota(jnp.int32, (r, n), 1)
        valid = (col % heads) == (row // dec)
        bias_ref[0] = jnp.where(valid, 0.0, NEG)
        rel = (past + row % dec) - ((n_pages - 1) * page + col // heads)
        bias_ref[1] = jnp.where(valid, near_bias(rel), NEG)
        row = lax.broadcasted_iota(jnp.int32, (r, r), 0)
        col = lax.broadcasted_iota(jnp.int32, (r, r), 1)
        rel = row % dec - col % dec
        valid = jnp.where((col // dec) == (row // dec), rel, -1) >= 0
        nbias_ref[...] = jnp.where(valid, near_bias(rel), NEG)

    @pl.when(p == 0)
    def _():
        m_ref[...] = jnp.full(m_ref.shape, NEG, F32)
        l_ref[...] = jnp.zeros(l_ref.shape, F32)
        acc_ref[...] = jnp.zeros(acc_ref.shape, F32)
        for c in range(2):
            qx_ref[c] = jnp.concatenate(
                [q_ref[:, (2 * h + c) * HEAD_DIM:(2 * h + c + 1) * HEAD_DIM] for h in range(heads)],
                axis=0).astype(BF16)

    def attend(tiles):
        scores = [[lax.dot_general(qx_ref[c], keys_of(c).astype(BF16), _NT,
                                   preferred_element_type=F32) + bias for keys_of, _, bias in tiles]
                  for c in range(2)]
        ps, alphas = [], []
        for c in range(2):
            rows = slice(c * r, (c + 1) * r)
            m_prev = m_ref[rows]
            m_new = m_prev
            for s in scores[c]:
                m_new = jnp.maximum(m_new, jnp.max(s, axis=-1, keepdims=True))
            alpha = jnp.exp2(m_prev - m_new)
            prs = [jnp.exp2(s - m_new) for s in scores[c]]
            rowsum = jnp.sum(prs[0], axis=-1, keepdims=True)
            for pr in prs[1:]:
                rowsum = rowsum + jnp.sum(pr, axis=-1, keepdims=True)
            l_ref[rows] = alpha * l_ref[rows] + rowsum
            m_ref[rows] = m_new
            ps.append([pr.astype(BF16) for pr in prs])
            alphas.append(alpha)
        pv = None
        for ti, (_, vb, _) in enumerate(tiles):
            part = jnp.dot(jnp.concatenate([ps[0][ti], ps[1][ti]], axis=0), vb,
                           preferred_element_type=F32)
            pv = part if pv is None else pv + part
        acc_ref[...] = jnp.concatenate(alphas, axis=0) * acc_ref[...] + pv

    def page_tiles(last_bias):
        tiles = []
        for gi in range(group):
            kc_ref = kc_refs[gi]
            bias = bias_ref[last_bias if gi == group - 1 else 0]
            tiles.append((lambda c, kc_ref=kc_ref: kc_ref[pl.ds(c, n, stride=2), :],
                          vc_refs[gi][...].astype(BF16), bias))
        return tiles

    @pl.when(p < n_groups - 1)
    def _():
        attend(page_tiles(0))

    @pl.when(p == n_groups - 1)
    def _():
        attend(page_tiles(1))

    @pl.when(p == n_groups)
    def _():
        def new_keys(c):
            return jnp.concatenate(
                [kn_ref[:, (2 * h + c) * HEAD_DIM:(2 * h + c + 1) * HEAD_DIM] for h in range(heads)],
                axis=0)
        vnew = jnp.concatenate([vn_ref[:, h * hw:(h + 1) * hw] for h in range(heads)], axis=0)
        attend([(new_keys, vnew.astype(BF16), nbias_ref[...])])
        lam = _lambda(lam_ref, lam_init)
        acc = acc_ref[...] / l_ref[...]
        for h in range(heads):
            sl = slice(h * hw, (h + 1) * hw)
            o_ref[:, sl] = _diff_out(acc[h * dec:(h + 1) * dec], acc[r + h * dec:r + (h + 1) * dec],
                                     lam, g_ref[...], z_ref[:, sl], lam_init).astype(o_ref.dtype)


def _attn_decode(aq, ak, av, z, z_col, cache_k, cache_v, page_table, rel_bias, da_lambda, da_subln, *,
                 batch, dec, heads, lam_init):
    hw = 2 * HEAD_DIM
    w = heads * hw
    n_pool, page = cache_k.shape[0], cache_k.shape[1]
    n_pages = page_table.shape[1]
    group = math.gcd(n_pages, PAGES_PER_STEP)
    n_groups = n_pages // group
    assert page >= REL_MAX_DIST and dec <= REL_MAX_DIST
    kc = cache_k.reshape(n_pool, page * heads * 2, HEAD_DIM)
    vc = cache_v.reshape(n_pool, page * heads, hw)
    r = heads * dec

    def cache_spec(rows, width, gi):
        return pl.BlockSpec(
            (None, rows, width),
            lambda b, p, pt: (pt[b, jnp.minimum(p, n_groups - 1) * group + gi], 0, 0))

    def row_spec(cb):
        return pl.BlockSpec((dec, w), lambda b, p, pt: (b, cb))

    grid_spec = pltpu.PrefetchScalarGridSpec(
        num_scalar_prefetch=1,
        grid=(batch, n_groups + 1),
        in_specs=([pl.BlockSpec(memory_space=pltpu.SMEM),
                   row_spec(0), row_spec(0), row_spec(0), row_spec(z_col // w)]
                  + [cache_spec(page * heads * 2, HEAD_DIM, gi) for gi in range(group)]
                  + [cache_spec(page * heads, hw, gi) for gi in range(group)]
                  + [pl.BlockSpec((4, HEAD_DIM), lambda b, p, pt: (0, 0)),
                     pl.BlockSpec((1, hw), lambda b, p, pt: (0, 0))]),
        out_specs=pl.BlockSpec((dec, w), lambda b, p, pt: (b, 0)),
        scratch_shapes=[pltpu.VMEM((2, r, HEAD_DIM), BF16),
                        pltpu.VMEM((2 * r, 1), F32), pltpu.VMEM((2 * r, 1), F32),
                        pltpu.VMEM((2 * r, hw), F32),
                        pltpu.VMEM((2, r, heads * page), F32), pltpu.VMEM((r, r), F32)],
    )
    return pl.pallas_call(
        functools.partial(_attn_decode_kernel, heads=heads, page=page, n_pages=n_pages, dec=dec,
                          group=group, lam_init=lam_init),
        grid_spec=grid_spec,
        out_shape=jax.ShapeDtypeStruct((batch * dec, w), F32),
        compiler_params=_params("arbitrary", "arbitrary"),
        name="attn_decode",
    )(page_table, rel_bias, aq, ak, av, z, *([kc] * group), *([vc] * group),
      da_lambda, da_subln.reshape(1, hw))


def _up_kernel(oh_ref, oa_ref, wh_ref, wa_ref, gh_ref, ga_ref, o_ref, whb_ref, wab_ref):
    @pl.when(pl.program_id(1) == 0)
    def _():
        whb_ref[...] = wh_ref[...].astype(BF16)
        wab_ref[...] = wa_ref[...].astype(BF16)

    uh = jnp.dot(oh_ref[...].astype(BF16), whb_ref[...], preferred_element_type=F32)
    ua = jnp.dot(oa_ref[...].astype(BF16), wab_ref[...], preferred_element_type=F32)
    o_ref[...] = (jax.nn.sigmoid(gh_ref[...]) * uh + jax.nn.sigmoid(ga_ref[...]) * ua).astype(o_ref.dtype)


def _up_merge(o_h, o_a, w_h, w_a, gates, gh_col, ga_col):
    m, kh = o_h.shape
    ka = o_a.shape[1]
    d = w_h.shape[1]
    tm, tn = _tile(m, ROW_TILE), _tile(d, 512)
    cgh, cga = gh_col // tn, ga_col // tn
    return pl.pallas_call(
        _up_kernel,
        grid=(d // tn, m // tm),
        in_specs=[pl.BlockSpec((tm, kh), lambda j, i: (i, 0)),
                  pl.BlockSpec((tm, ka), lambda j, i: (i, 0)),
                  pl.BlockSpec((kh, tn), lambda j, i: (0, j)),
                  pl.BlockSpec((ka, tn), lambda j, i: (0, j)),
                  pl.BlockSpec((tm, tn), lambda j, i: (i, cgh + j)),
                  pl.BlockSpec((tm, tn), lambda j, i: (i, cga + j))],
        out_specs=pl.BlockSpec((tm, tn), lambda j, i: (i, j)),
        out_shape=jax.ShapeDtypeStruct((m, d), BF16),
        scratch_shapes=[pltpu.VMEM((kh, tn), BF16), pltpu.VMEM((ka, tn), BF16)],
        compiler_params=_params("arbitrary", "arbitrary"),
        name="up_merge",
    )(o_h, o_a, w_h, w_a, gates, gates)


def _post_kernel(x_ref, o_ref, g_ref, y_ref):
    out = o_ref[...]
    ms = jnp.mean(out * out, axis=-1, keepdims=True)
    y_ref[...] = x_ref[...] + out * lax.rsqrt(ms + EPS) * g_ref[...]


def _out_proj(merged, w_out, x, g_post):
    m, d = x.shape
    (out,) = _proj(merged, w_out, 0, d, [(F32, 1.0, "rows")], tn_pref=1024, name="out_proj")
    tm = _tile(m, 256)
    return pl.pallas_call(
        _post_kernel,
        grid=(m // tm,),
        in_specs=[pl.BlockSpec((tm, d), lambda i: (i, 0)),
                  pl.BlockSpec((tm, d), lambda i: (i, 0)),
                  pl.BlockSpec((1, d), lambda i: (0, 0))],
        out_specs=pl.BlockSpec((tm, d), lambda i: (i, 0)),
        out_shape=jax.ShapeDtypeStruct((m, d), F32),
        compiler_params=_params("arbitrary"),
        name="post_norm",
    )(x, out, g_post.reshape(1, d))


def _layer(x, s0, paged, w, *, layer, lam_init):
    bsz, seq, d = x.shape
    m = bsz * seq
    hg_heads = (d // 2) // HEAD_DIM
    da_heads = (d // 2) // (2 * HEAD_DIM)
    hg_w = hg_heads * HEAD_DIM
    da_w = da_heads * 2 * HEAD_DIM
    col_aq = 4 * hg_w
    prompt = paged is None
    qscale = LOG2E * HEAD_DIM ** -0.5

    x2 = x.reshape(m, d)
    xn = _rmsnorm_cast(x2, w["g_pre"])
    proj = functools.partial(_proj, xn, w["w_in"], tn_pref=512, name="in_proj")
    (hg,) = proj(0, 4 * hg_w, [(F32, 1.0, "rows")])
    (aq,) = proj(col_aq, da_w, [(BF16 if prompt else F32, qscale, "rows")])
    k_outs = proj(col_aq + da_w, da_w, [(F32, 1.0, "rows")] + ([(BF16, 1.0, "rows")] if prompt else []))
    v_outs = proj(col_aq + 2 * da_w, da_w,
                  [(F32, 1.0, "rows")] + ([(BF16, 1.0, "tiles_t")] if prompt else []))
    (gates,) = proj(col_aq + 3 * da_w, da_w + 2 * d, [(F32, 1.0, "rows")])

    if prompt:
        chunk, hb, act_dtype = math.gcd(seq, 256), 8, BF16
    else:
        chunk, hb, act_dtype = math.gcd(seq, 64), 4, F32
    assert chunk % 16 == 0 or act_dtype == F32
    o_h, s_new = _hgrn(hg, w["hg_lower"], w["hg_norm"], s0, batch=bsz, seq=seq, heads=hg_heads,
                       layer=layer, chunk=chunk, hb=math.gcd(hg_heads, hb), out_dtype=act_dtype)

    if prompt:
        assert bsz == 1
        o_a = _attn_prompt(aq, k_outs[1], v_outs[1], gates, 0, w["rel_bias"], w["da_lambda"],
                           w["da_subln"], seq=seq, heads=da_heads, lam_init=lam_init)
    else:
        cache_k, cache_v, page_table = paged
        o_a = _attn_decode(aq, k_outs[0], v_outs[0], gates, 0, cache_k, cache_v, page_table,
                           w["rel_bias"], w["da_lambda"], w["da_subln"], batch=bsz, dec=seq,
                           heads=da_heads, lam_init=lam_init)

    merged = _up_merge(o_h, o_a, w["w_up_hg"], w["w_up_da"], gates, da_w, da_w + d)
    y = _out_proj(merged, w["w_out"], x2, w["g_post"])

    k_new = k_outs[0].reshape(bsz, seq, da_heads, 2, HEAD_DIM)
    v_new = v_outs[0].reshape(bsz, seq, da_heads, 2 * HEAD_DIM)
    return y.reshape(bsz, seq, d), k_new, v_new, s_new


def kernel(x_prompt, x_sample, cache_k, cache_v, state_hgrn, page_table, norm_pre, norm_post, w_in,
           hg_lower, hg_norm, da_lambda, da_subln, rel_bias, w_up_hg, w_up_da, w_out):
    depth = w_in.shape[0]
    hp, hs = x_prompt, x_sample
    outs = [[] for _ in range(6)]
    for l in range(depth):
        lam_init = 0.8 - 0.6 * math.exp(-0.3 * l)
        w = dict(g_pre=norm_pre[l], g_post=norm_post[l], w_in=w_in[l], hg_lower=hg_lower,
                 hg_norm=hg_norm[l], da_lambda=da_lambda[l], da_subln=da_subln[l], rel_bias=rel_bias,
                 w_up_hg=w_up_hg[l], w_up_da=w_up_da[l], w_out=w_out[l].astype(BF16))
        hg_heads = (hp.shape[-1] // 2) // HEAD_DIM
        s0p = jnp.zeros((hp.shape[0], hg_heads, HEAD_DIM, HEAD_DIM), F32)
        hp, kp, vp, sp = _layer(hp, s0p, None, w, layer=l, lam_init=lam_init)
        hs, ks, vs, ss = _layer(hs, state_hgrn[l], (cache_k[l], cache_v[l], page_table), w,
                                layer=l, lam_init=lam_init)
        for lst, val in zip(outs, (kp, vp, sp, ks, vs, ss)):
            lst.append(val)
    return (hp, hs) + tuple(jnp.stack(o) for o in outs)
```

```python
import functools
import math

import jax
import jax.numpy as jnp
from jax import lax
from jax.experimental import pallas as pl
from jax.experimental.pallas import tpu as pltpu

HEAD_DIM = 128
REL_BUCKETS = 32
REL_MAX_DIST = 128
EPS = 1e-6
NEG = -1e30
LOG2E = math.log2(math.e)
VMEM_LIMIT = 56 * 1024 * 1024
BF16 = jnp.bfloat16
F32 = jnp.float32

_NT = (((1,), (1,)), ((), ()))
_TN = (((0,), (0,)), ((), ()))


def _params(*sem):
    return pltpu.CompilerParams(dimension_semantics=sem, vmem_limit_bytes=VMEM_LIMIT)


def _silu(z):
    return z * jax.nn.sigmoid(z)


ROW_TILE = 1024


def _tile(n, pref):
    t = min(n, pref)
    assert n % t == 0, (n, t)
    return t


def _rmsnorm_kernel(x_ref, g_ref, o_ref):
    x = x_ref[...]
    ms = jnp.mean(x * x, axis=-1, keepdims=True)
    o_ref[...] = (x * lax.rsqrt(ms + EPS) * g_ref[...]).astype(o_ref.dtype)


def _rmsnorm_cast(x, g):
    m, d = x.shape
    tm = _tile(m, 256)
    return pl.pallas_call(
        _rmsnorm_kernel,
        grid=(m // tm,),
        in_specs=[pl.BlockSpec((tm, d), lambda i: (i, 0)),
                  pl.BlockSpec((1, d), lambda i: (0, 0))],
        out_specs=pl.BlockSpec((tm, d), lambda i: (i, 0)),
        out_shape=jax.ShapeDtypeStruct((m, d), BF16),
        compiler_params=_params("arbitrary"),
        name="rmsnorm_cast",
    )(x, g.reshape(1, d))


def _store_products(acc, out_refs, outs):
    for o_ref, (_, scale, layout) in zip(out_refs, outs):
        val = acc if scale == 1.0 else acc * scale
        o_ref[...] = (val.T if layout == "tiles_t" else val).astype(o_ref.dtype)


def _proj_kernel(*refs, outs, outs2, cast):
    refs = list(refs)
    a_ref = refs.pop(0)
    a2_ref = refs.pop(0) if outs2 else None
    w_ref = refs.pop(0)
    wb_ref = refs.pop() if cast else None
    out_refs, out2_refs = refs[:len(outs)], refs[len(outs):]
    first = pl.program_id(1) == 0

    if cast:
        @pl.when(first)
        def _():
            wb_ref[...] = w_ref[...].astype(BF16)
        wb = wb_ref[...]
    else:
        wb = w_ref[...]

    _store_products(jnp.dot(a_ref[...], wb, preferred_element_type=F32), out_refs, outs)
    if outs2:
        @pl.when(first)
        def _():
            _store_products(jnp.dot(a2_ref[...], wb, preferred_element_type=F32), out2_refs, outs2)


def _proj(a, w, col_start, width, outs, *, tn_pref, name, a2=None, outs2=()):
    m, kd = a.shape
    tm, tn = _tile(m, ROW_TILE), _tile(width, tn_pref)
    assert col_start % tn == 0
    c0 = col_start // tn
    cast = w.dtype != BF16
    outs, outs2 = tuple(outs), tuple(outs2)
    assert all(layout == "rows" for _, _, layout in outs2)

    def out_spec(layout):
        if layout == "tiles_t":
            return pl.BlockSpec((None, tn, tm), lambda j, i: (i, j, 0))
        return pl.BlockSpec((tm, tn), lambda j, i: (i, j))

    def out_shape(dt, layout):
        return jax.ShapeDtypeStruct((m // tm, width, tm) if layout == "tiles_t" else (m, width), dt)

    in_specs = [pl.BlockSpec((tm, kd), lambda j, i: (i, 0))]
    operands = [a]
    if outs2:
        m2 = a2.shape[0]
        in_specs.append(pl.BlockSpec((m2, kd), lambda j, i: (0, 0)))
        operands.append(a2)
    in_specs.append(pl.BlockSpec((kd, tn), lambda j, i: (0, c0 + j)))
    operands.append(w)
    return pl.pallas_call(
        functools.partial(_proj_kernel, outs=outs, outs2=outs2, cast=cast),
        grid=(width // tn, m // tm),
        in_specs=in_specs,
        out_specs=([out_spec(layout) for _, _, layout in outs]
                   + [pl.BlockSpec((a2.shape[0], tn), lambda j, i: (0, j)) for _ in outs2]),
        out_shape=([out_shape(dt, layout) for dt, _, layout in outs]
                   + [jax.ShapeDtypeStruct((a2.shape[0], width), dt) for dt, _, _ in outs2]),
        scratch_shapes=[pltpu.VMEM((kd, tn), BF16)] if cast else [],
        compiler_params=_params("arbitrary", "arbitrary"),
        name=name,
    )(*operands)


def _hgrn_masks(c):
    row = lax.broadcasted_iota(jnp.int32, (c, HEAD_DIM), 0)
    ti = lax.broadcasted_iota(jnp.int32, (c, c), 0)
    si = lax.broadcasted_iota(jnp.int32, (c, c), 1)
    scan, levels = [], []
    sh = 1
    while sh < c:
        scan.append((sh, row >= sh))
        sh *= 2
    n, lg = 1, 0
    while n < c:
        u = ti >> lg
        w = si >> lg
        levels.append((n, ((u ^ w) * 2 + (u & 1)) == 3, (row & n) != 0))
        n *= 2
        lg += 1
    return scan, levels


def _hgrn_head(q, fr, v, z, lo, g, st, masks, *, c, layer):
    scan, levels = masks
    e = jnp.exp(lo - jnp.max(lo, axis=0, keepdims=True))
    lb = jnp.sum(e[:layer + 1], axis=0, keepdims=True) / jnp.sum(e, axis=0, keepdims=True)
    f = lb + (1.0 - lb) * jax.nn.sigmoid(fr)
    k = 1.0 - f

    b = jnp.log2(f)
    for sh, keep in scan:
        b = b + jnp.where(keep, pltpu.roll(b, sh, 0), 0.0)
    b_last = b[c - 1:c, :]

    vb = v.astype(BF16)
    o = lax.dot_general((q * jnp.exp2(b)).astype(BF16), st.astype(BF16), _NT,
                        preferred_element_type=F32)

    att = jnp.zeros((c, c), F32)
    bend = b
    for n, pair_mask, upper in levels:
        bstart = pltpu.roll(bend, n, 0)
        qn = (q * jnp.exp2(b - bstart)).astype(BF16)
        kn = (k * jnp.exp2(bend - b)).astype(BF16)
        an = lax.dot_general(qn, kn, _NT, preferred_element_type=F32)
        att = jnp.where(pair_mask, an, att)
        bend = jnp.where(upper, bend, pltpu.roll(bend, c - n, 0))
    diag = jnp.sum(q * k, axis=-1, keepdims=True)
    o = o + jnp.dot(att.astype(BF16), vb, preferred_element_type=F32) + diag * v

    ke = (k * jnp.exp2(b_last - b)).astype(BF16)
    st_new = jnp.exp2(b_last) * st + lax.dot_general(vb, ke, _TN, preferred_element_type=F32)
    on = o * lax.rsqrt(jnp.mean(o * o, axis=-1, keepdims=True) + EPS) * g
    return on * _silu(z), st_new


def _hgrn_kernel(q_ref, f_ref, i_ref, z_ref, lo_ref, g_ref, s0_ref, o_ref, s_ref, st_ref, *,
                 chunk, layer, hb):
    step = pl.program_id(2)
    last = step == pl.num_programs(2) - 1
    masks = _hgrn_masks(chunk)
    for hh in range(hb):
        sl = slice(hh * HEAD_DIM, (hh + 1) * HEAD_DIM)

        @pl.when(step == 0)
        def _():
            st_ref[hh] = s0_ref[0, hh].T

        out, st_new = _hgrn_head(q_ref[:, sl], f_ref[:, sl], i_ref[:, sl], z_ref[:, sl],
                                 lo_ref[:, sl], g_ref[...], st_ref[hh], masks, c=chunk, layer=layer)
        st_ref[hh] = st_new
        o_ref[:, sl] = out.astype(o_ref.dtype)

        @pl.when(last)
        def _():
            s_ref[0, hh] = st_new.T


def _hgrn(hg, *, hg_lower, hg_norm, s0, batch, seq, heads, layer, chunk, hb, out_dtype):
    nc = seq // chunk
    bw = hb * HEAD_DIM
    nhb = heads // hb

    def col(t):
        return pl.BlockSpec((chunk, bw), lambda b, h, n: (b * nc + n, t * nhb + h))

    state_spec = pl.BlockSpec((1, hb, HEAD_DIM, HEAD_DIM), lambda b, h, n: (b, h, 0, 0))
    return pl.pallas_call(
        functools.partial(_hgrn_kernel, chunk=chunk, layer=layer, hb=hb),
        grid=(batch, nhb, nc),
        in_specs=[col(0), col(1), col(2), col(3),
                  pl.BlockSpec((hg_lower.shape[0], bw), lambda b, h, n: (0, h)),
                  pl.BlockSpec((1, HEAD_DIM), lambda b, h, n: (0, 0)),
                  state_spec],
        out_specs=[pl.BlockSpec((chunk, bw), lambda b, h, n: (b * nc + n, h)), state_spec],
        out_shape=[jax.ShapeDtypeStruct((batch * seq, heads * HEAD_DIM), out_dtype),
                   jax.ShapeDtypeStruct((batch, heads, HEAD_DIM, HEAD_DIM), F32)],
        scratch_shapes=[pltpu.VMEM((hb, HEAD_DIM, HEAD_DIM), F32)],
        compiler_params=_params("arbitrary", "arbitrary", "arbitrary"),
        name="hgrn2",
    )(hg, hg, hg, hg, hg_lower, hg_norm.reshape(1, HEAD_DIM), s0)


def _rel_bias_tile(rel, rb_ref, h):
    max_exact = REL_BUCKETS // 2
    n = jnp.maximum(rel, 0)
    nf = jnp.maximum(n, 1).astype(F32)
    large = max_exact + (jnp.log(nf / max_exact) / math.log(REL_MAX_DIST / max_exact)
                         * (REL_BUCKETS - max_exact)).astype(jnp.int32)
    bucket = jnp.where(n < max_exact, n, jnp.minimum(large, REL_BUCKETS - 1))
    far = rb_ref[REL_BUCKETS - 1, h]
    tile = jnp.zeros(rel.shape, F32)
    for bk in range(REL_BUCKETS - 1):
        tile = jnp.where(bucket == bk, (rb_ref[bk, h] - far) * LOG2E, tile)
    return tile


def _lambda(lam_ref, lam_init):
    lv = lam_ref[...]
    return (jnp.exp(jnp.sum(lv[0:1] * lv[1:2], axis=-1, keepdims=True))
            - jnp.exp(jnp.sum(lv[2:3] * lv[3:4], axis=-1, keepdims=True)) + lam_init)


def _diff_out(o1, o2, lam, g, z, lam_init):
    o = o1 - lam * o2
    on = o * lax.rsqrt(jnp.mean(o * o, axis=-1, keepdims=True) + EPS) * g * (1.0 - lam_init)
    return on * _silu(z)


def _attn_prompt_kernel(rb_ref, q_ref, k_ref, vt_ref, z_ref, lam_ref, g_ref, o_ref,
                        m_ref, l_ref, acc_ref, bias_ref, *, tile, lam_init):
    t = tile
    h, i = pl.program_id(0), pl.program_id(1)

    @pl.when(i == 0)
    def _():
        si = lax.broadcasted_iota(jnp.int32, (t, t), 0)
        ti = lax.broadcasted_iota(jnp.int32, (t, t), 1)
        bias_ref[0] = jnp.where(si <= ti, _rel_bias_tile(ti - si, rb_ref, h), NEG)
        bias_ref[1] = _rel_bias_tile(ti - si + t, rb_ref, h)

    m_ref[...] = jnp.full(m_ref.shape, NEG, F32)
    l_ref[...] = jnp.zeros(l_ref.shape, F32)
    acc_ref[...] = jnp.zeros(acc_ref.shape, F32)

    def update(j, bias):
        k = k_ref[j]
        vt = vt_ref[j]
        sts = []
        for c in range(2):
            sl = slice(c * HEAD_DIM, (c + 1) * HEAD_DIM)
            st = lax.dot_general(k[:, sl], q_ref[:, sl], _NT, preferred_element_type=F32)
            sts.append(st if bias is None else st + bias)
        ps = []
        for c in range(2):
            m_prev = m_ref[c]
            m_new = jnp.maximum(m_prev, jnp.max(sts[c], axis=0, keepdims=True))
            alpha = jnp.exp2(m_prev - m_new)
            p = jnp.exp2(sts[c] - m_new)
            l_ref[c] = alpha * l_ref[c] + jnp.sum(p, axis=0, keepdims=True)
            m_ref[c] = m_new
            ps.append((alpha, p.astype(BF16)))
        for c in range(2):
            alpha, pb = ps[c]
            acc_ref[c] = alpha * acc_ref[c] + jnp.dot(vt, pb, preferred_element_type=F32)

    def update2(j0):
        kk = [k_ref[j0], k_ref[j0 + 1]]
        vts = [vt_ref[j0], vt_ref[j0 + 1]]
        sts = [[lax.dot_general(kk[u][:, c * HEAD_DIM:(c + 1) * HEAD_DIM],
                                q_ref[:, c * HEAD_DIM:(c + 1) * HEAD_DIM], _NT,
                                preferred_element_type=F32) for u in range(2)] for c in range(2)]
        ps = []
        for c in range(2):
            m_prev = m_ref[c]
            m_new = jnp.maximum(m_prev, jnp.maximum(jnp.max(sts[c][0], axis=0, keepdims=True),
                                                    jnp.max(sts[c][1], axis=0, keepdims=True)))
            alpha = jnp.exp2(m_prev - m_new)
            p0 = jnp.exp2(sts[c][0] - m_new)
            p1 = jnp.exp2(sts[c][1] - m_new)
            l_ref[c] = (alpha * l_ref[c] + jnp.sum(p0, axis=0, keepdims=True)
                        + jnp.sum(p1, axis=0, keepdims=True))
            m_ref[c] = m_new
            ps.append((alpha, p0.astype(BF16), p1.astype(BF16)))
        for c in range(2):
            alpha, p0, p1 = ps[c]
            acc_ref[c] = (alpha * acc_ref[c] + jnp.dot(vts[0], p0, preferred_element_type=F32)
                          + jnp.dot(vts[1], p1, preferred_element_type=F32))

    n_far = jnp.maximum(i - 1, 0)

    def far_pair(jj, carry):
        update2(2 * jj)
        return carry

    lax.fori_loop(0, n_far // 2, far_pair, 0)

    @pl.when(n_far % 2 == 1)
    def _():
        update(n_far - 1, None)

    @pl.when(i >= 1)
    def _():
        update(i - 1, bias_ref[1])

    update(i, bias_ref[0])
    lam = _lambda(lam_ref, lam_init)
    ot = acc_ref[0] / l_ref[0] - lam * (acc_ref[1] / l_ref[1])
    o = ot.T
    on = o * lax.rsqrt(jnp.mean(o * o, axis=-1, keepdims=True) + EPS) * g_ref[...] * (1.0 - lam_init)
    o_ref[...] = (on * _silu(z_ref[...])).astype(o_ref.dtype)


def _attn_prompt(aq, ak, avt, z, z_col, rel_bias, da_lambda, da_subln, *, seq, heads, lam_init):
    hw = 2 * HEAD_DIM
    nq, _, tile = avt.shape
    assert tile >= REL_MAX_DIST and nq * tile == seq
    cz = z_col // hw
    return pl.pallas_call(
        functools.partial(_attn_prompt_kernel, tile=tile, lam_init=lam_init),
        grid=(heads, nq),
        in_specs=[pl.BlockSpec(memory_space=pltpu.SMEM),
                  pl.BlockSpec((tile, hw), lambda h, i: (i, h)),
                  pl.BlockSpec((nq, tile, hw), lambda h, i: (0, 0, h)),
                  pl.BlockSpec((nq, hw, tile), lambda h, i: (0, h, 0)),
                  pl.BlockSpec((tile, hw), lambda h, i: (i, cz + h)),
                  pl.BlockSpec((4, HEAD_DIM), lambda h, i: (0, 0)),
                  pl.BlockSpec((1, hw), lambda h, i: (0, 0))],
        out_specs=pl.BlockSpec((tile, hw), lambda h, i: (i, h)),
        out_shape=jax.ShapeDtypeStruct((seq, heads * hw), BF16),
        scratch_shapes=[pltpu.VMEM((2, 1, tile), F32), pltpu.VMEM((2, 1, tile), F32),
                        pltpu.VMEM((2, hw, tile), F32), pltpu.VMEM((2, tile, tile), F32)],
        compiler_params=_params("arbitrary", "arbitrary"),
        name="attn_prompt",
    )(rel_bias, aq, ak.reshape(nq, tile, heads * hw), avt, z, da_lambda, da_subln.reshape(1, hw))


PAGES_PER_STEP = 8


def _attn_decode_kernel(pt_ref, rb_ref, q_ref, kn_ref, vn_ref, z_ref, *rest,
                        heads, page, n_pages, dec, group, lam_init):
    del pt_ref
    kc_refs, vc_refs = rest[:group], rest[group:2 * group]
    lam_ref, g_ref, o_ref, qx_ref, m_ref, l_ref, acc_ref, bias_ref, nbias_ref = rest[2 * group:]
    b, p = pl.program_id(0), pl.program_id(1)
    hw = 2 * HEAD_DIM
    r = heads * dec
    n = heads * page
    n_groups = n_pages // group
    past = n_pages * page

    def near_bias(rel):
        return jnp.concatenate([_rel_bias_tile(rel[h * dec:(h + 1) * dec], rb_ref, h)
                                for h in range(heads)], axis=0)

    @pl.when((b == 0) & (p == 0))
    def _():
        row = lax.broadcasted_iota(jnp.int32, (r, n), 0)
        col = lax.broadcasted_iota(jnp.int32, (r, n), 1)
        valid = (col % heads) == (row // dec)
        bias_ref[0] = jnp.where(valid, 0.0, NEG)
        rel = (past + row % dec) - ((n_pages - 1) * page + col // heads)
        bias_ref[1] = jnp.where(valid, near_bias(rel), NEG)
        row = lax.broadcasted_iota(jnp.int32, (r, r), 0)
        col = lax.broadcasted_iota(jnp.int32, (r, r), 1)
        rel = row % dec - col % dec
        valid = jnp.where((col // dec) == (row // dec), rel, -1) >= 0
        nbias_ref[...] = jnp.where(valid, near_bias(rel), NEG)

    @pl.when(p == 0)
    def _():
        m_ref[...] = jnp.full(m_ref.shape, NEG, F32)
        l_ref[...] = jnp.zeros(l_ref.shape, F32)
        acc_ref[...] = jnp.zeros(acc_ref.shape, F32)
        for c in range(2):
            qx_ref[c] = jnp.concatenate(
                [q_ref[:, (2 * h + c) * HEAD_DIM:(2 * h + c + 1) * HEAD_DIM] for h in range(heads)],
                axis=0).astype(BF16)

    def attend(tiles):
        scores = [[lax.dot_general(qx_ref[c], keys_of(c).astype(BF16), _NT,
                                   preferred_element_type=F32) + bias for keys_of, _, bias in tiles]
                  for c in range(2)]
        ps, alphas = [], []
        for c in range(2):
            rows = slice(c * r, (c + 1) * r)
            m_prev = m_ref[rows]
            m_new = m_prev
            for s in scores[c]:
                m_new = jnp.maximum(m_new, jnp.max(s, axis=-1, keepdims=True))
            alpha = jnp.exp2(m_prev - m_new)
            prs = [jnp.exp2(s - m_new) for s in scores[c]]
            rowsum = jnp.sum(prs[0], axis=-1, keepdims=True)
            for pr in prs[1:]:
                rowsum = rowsum + jnp.sum(pr, axis=-1, keepdims=True)
            l_ref[rows] = alpha * l_ref[rows] + rowsum
            m_ref[rows] = m_new
            ps.append([pr.astype(BF16) for pr in prs])
            alphas.append(alpha)
        pv = None
        for ti, (_, vb, _) in enumerate(tiles):
            part = jnp.dot(jnp.concatenate([ps[0][ti], ps[1][ti]], axis=0), vb,
                           preferred_element_type=F32)
            pv = part if pv is None else pv + part
        acc_ref[...] = jnp.concatenate(alphas, axis=0) * acc_ref[...] + pv

    def page_tiles(last_bias):
        tiles = []
        for gi in range(group):
            kc_ref = kc_refs[gi]
            bias = bias_ref[last_bias if gi == group - 1 else 0]
            tiles.append((lambda c, kc_ref=kc_ref: kc_ref[pl.ds(c, n, stride=2), :],
                          vc_refs[gi][...].astype(BF16), bias))
        return tiles

    @pl.when(p < n_groups - 1)
    def _():
        attend(page_tiles(0))

    @pl.when(p == n_groups - 1)
    def _():
        attend(page_tiles(1))

    @pl.when(p == n_groups)
    def _():
        def new_keys(c):
            return jnp.concatenate(
                [kn_ref[:, (2 * h + c) * HEAD_DIM:(2 * h + c + 1) * HEAD_DIM] for h in range(heads)],
                axis=0)
        vnew = jnp.concatenate([vn_ref[:, h * hw:(h + 1) * hw] for h in range(heads)], axis=0)
        attend([(new_keys, vnew.astype(BF16), nbias_ref[...])])
        lam = _lambda(lam_ref, lam_init)
        acc = acc_ref[...] / l_ref[...]
        for h in range(heads):
            sl = slice(h * hw, (h + 1) * hw)
            o_ref[:, sl] = _diff_out(acc[h * dec:(h + 1) * dec], acc[r + h * dec:r + (h + 1) * dec],
                                     lam, g_ref[...], z_ref[:, sl], lam_init).astype(o_ref.dtype)


def _attn_decode(aq, ak, av, z, z_col, cache_k, cache_v, page_table, rel_bias, da_lambda, da_subln, *,
                 batch, dec, heads, lam_init):
    hw = 2 * HEAD_DIM
    w = heads * hw
    n_pool, page = cache_k.shape[0], cache_k.shape[1]
    n_pages = page_table.shape[1]
    group = math.gcd(n_pages, PAGES_PER_STEP)
    n_groups = n_pages // group
    assert page >= REL_MAX_DIST and dec <= REL_MAX_DIST
    kc = cache_k.reshape(n_pool, page * heads * 2, HEAD_DIM)
    vc = cache_v.reshape(n_pool, page * heads, hw)
    r = heads * dec

    def cache_spec(rows, width, gi):
        return pl.BlockSpec(
            (None, rows, width),
            lambda b, p, pt: (pt[b, jnp.minimum(p, n_groups - 1) * group + gi], 0, 0))

    def row_spec(cb):
        return pl.BlockSpec((dec, w), lambda b, p, pt: (b, cb))

    grid_spec = pltpu.PrefetchScalarGridSpec(
        num_scalar_prefetch=1,
        grid=(batch, n_groups + 1),
        in_specs=([pl.BlockSpec(memory_space=pltpu.SMEM),
                   row_spec(0), row_spec(0), row_spec(0), row_spec(z_col // w)]
                  + [cache_spec(page * heads * 2, HEAD_DIM, gi) for gi in range(group)]
                  + [cache_spec(page * heads, hw, gi) for gi in range(group)]
                  + [pl.BlockSpec((4, HEAD_DIM), lambda b, p, pt: (0, 0)),
                     pl.BlockSpec((1, hw), lambda b, p, pt: (0, 0))]),
        out_specs=pl.BlockSpec((dec, w), lambda b, p, pt: (b, 0)),
        scratch_shapes=[pltpu.VMEM((2, r, HEAD_DIM), BF16),
                        pltpu.VMEM((2 * r, 1), F32), pltpu.VMEM((2 * r, 1), F32),
                        pltpu.VMEM((2 * r, hw), F32),
                        pltpu.VMEM((2, r, heads * page), F32), pltpu.VMEM((r, r), F32)],
    )
    return pl.pallas_call(
        functools.partial(_attn_decode_kernel, heads=heads, page=page, n_pages=n_pages, dec=dec,
                          group=group, lam_init=lam_init),
        grid_spec=grid_spec,
        out_shape=jax.ShapeDtypeStruct((batch * dec, w), F32),
        compiler_params=_params("arbitrary", "arbitrary"),
        name="attn_decode",
    )(page_table, rel_bias, aq, ak, av, z, *([kc] * group), *([vc] * group),
      da_lambda, da_subln.reshape(1, hw))


def _up_kernel(oh_ref, oa_ref, wh_ref, wa_ref, gh_ref, ga_ref, o_ref, whb_ref, wab_ref):
    @pl.when(pl.program_id(1) == 0)
    def _():
        whb_ref[...] = wh_ref[...].astype(BF16)
        wab_ref[...] = wa_ref[...].astype(BF16)

    uh = jnp.dot(oh_ref[...].astype(BF16), whb_ref[...], preferred_element_type=F32)
    ua = jnp.dot(oa_ref[...].astype(BF16), wab_ref[...], preferred_element_type=F32)
    o_ref[...] = (jax.nn.sigmoid(gh_ref[...]) * uh + jax.nn.sigmoid(ga_ref[...]) * ua).astype(o_ref.dtype)


def _up_merge(o_h, o_a, w_h, w_a, gates, gh_col, ga_col):
    m, kh = o_h.shape
    ka = o_a.shape[1]
    d = w_h.shape[1]
    tm, tn = _tile(m, ROW_TILE), _tile(d, 512)
    cgh, cga = gh_col // tn, ga_col // tn
    return pl.pallas_call(
        _up_kernel,
        grid=(d // tn, m // tm),
        in_specs=[pl.BlockSpec((tm, kh), lambda j, i: (i, 0)),
                  pl.BlockSpec((tm, ka), lambda j, i: (i, 0)),
                  pl.BlockSpec((kh, tn), lambda j, i: (0, j)),
                  pl.BlockSpec((ka, tn), lambda j, i: (0, j)),
                  pl.BlockSpec((tm, tn), lambda j, i: (i, cgh + j)),
                  pl.BlockSpec((tm, tn), lambda j, i: (i, cga + j))],
        out_specs=pl.BlockSpec((tm, tn), lambda j, i: (i, j)),
        out_shape=jax.ShapeDtypeStruct((m, d), BF16),
        scratch_shapes=[pltpu.VMEM((kh, tn), BF16), pltpu.VMEM((ka, tn), BF16)],
        compiler_params=_params("arbitrary", "arbitrary"),
        name="up_merge",
    )(o_h, o_a, w_h, w_a, gates, gates)


def _out_kernel(a_ref, w_ref, x_ref, g_ref, y_ref):
    out = jnp.dot(a_ref[...], w_ref[...], preferred_element_type=F32)
    ms = jnp.mean(out * out, axis=-1, keepdims=True)
    y_ref[...] = x_ref[...] + out * lax.rsqrt(ms + EPS) * g_ref[...]


def _out_proj(merged, w_out, x, g_post):
    m, d = x.shape
    tm = _tile(m, 128)
    return pl.pallas_call(
        _out_kernel,
        grid=(m // tm,),
        in_specs=[pl.BlockSpec((tm, d), lambda i: (i, 0)),
                  pl.BlockSpec((d, d), lambda i: (0, 0), pipeline_mode=pl.Buffered(1)),
                  pl.BlockSpec((tm, d), lambda i: (i, 0)),
                  pl.BlockSpec((1, d), lambda i: (0, 0))],
        out_specs=pl.BlockSpec((tm, d), lambda i: (i, 0)),
        out_shape=jax.ShapeDtypeStruct((m, d), F32),
        compiler_params=_params("arbitrary"),
        name="out_proj",
    )(merged, w_out, x, g_post.reshape(1, d))


def _mix_and_project(x2, o_h, o_a, gates, w, da_w):
    d = x2.shape[1]
    merged = _up_merge(o_h, o_a, w["w_up_hg"], w["w_up_da"], gates, da_w, da_w + d)
    return _out_proj(merged, w["w_out"], x2, w["g_post"])


def _layer(xp, xs, s0s, cache_k, cache_v, page_table, w, *, layer, lam_init):
    _, seq, d = xp.shape
    bsz, dec, _ = xs.shape
    assert xp.shape[0] == 1
    hg_heads = (d // 2) // HEAD_DIM
    da_heads = (d // 2) // (2 * HEAD_DIM)
    hg_w = hg_heads * HEAD_DIM
    da_w = da_heads * 2 * HEAD_DIM
    col_aq = 4 * hg_w
    qscale = LOG2E * HEAD_DIM ** -0.5
    f32 = (F32, 1.0, "rows")

    xp2, xs2 = xp.reshape(seq, d), xs.reshape(bsz * dec, d)
    xnp, xns = _rmsnorm_cast(xp2, w["g_pre"]), _rmsnorm_cast(xs2, w["g_pre"])
    proj = functools.partial(_proj, xnp, w["w_in"], tn_pref=512, name="in_proj", a2=xns)
    hg_p, hg_s = proj(0, 4 * hg_w, [f32], outs2=[f32])
    aq_p, aq_s = proj(col_aq, da_w, [(BF16, qscale, "rows")], outs2=[(F32, qscale, "rows")])
    k_p, k16_p, k_s = proj(col_aq + da_w, da_w, [f32, (BF16, 1.0, "rows")], outs2=[f32])
    v_p, vt16_p, v_s = proj(col_aq + 2 * da_w, da_w, [f32, (BF16, 1.0, "tiles_t")], outs2=[f32])
    gates_p, gates_s = proj(col_aq + 3 * da_w, da_w + 2 * d, [f32], outs2=[f32])

    s0p = jnp.zeros((1, hg_heads, HEAD_DIM, HEAD_DIM), F32)
    hgrn = functools.partial(_hgrn, hg_lower=w["hg_lower"], hg_norm=w["hg_norm"], heads=hg_heads, layer=layer)
    oh_p, sp = hgrn(hg_p, s0=s0p, batch=1, seq=seq, chunk=math.gcd(seq, 256), hb=math.gcd(hg_heads, 8),
                    out_dtype=BF16)
    oh_s, ss = hgrn(hg_s, s0=s0s, batch=bsz, seq=dec, chunk=math.gcd(dec, 64), hb=math.gcd(hg_heads, 16),
                    out_dtype=F32)

    attn_w = (w["rel_bias"], w["da_lambda"], w["da_subln"])
    oa_p = _attn_prompt(aq_p, k16_p, vt16_p, gates_p, 0, *attn_w, seq=seq, heads=da_heads, lam_init=lam_init)
    oa_s = _attn_decode(aq_s, k_s, v_s, gates_s, 0, cache_k, cache_v, page_table, *attn_w, batch=bsz,
                        dec=dec, heads=da_heads, lam_init=lam_init)

    yp = _mix_and_project(xp2, oh_p, oa_p, gates_p, w, da_w).reshape(xp.shape)
    ys = _mix_and_project(xs2, oh_s, oa_s, gates_s, w, da_w).reshape(xs.shape)
    kv_p = (k_p.reshape(1, seq, da_heads, 2, HEAD_DIM), v_p.reshape(1, seq, da_heads, 2 * HEAD_DIM))
    kv_s = (k_s.reshape(bsz, dec, da_heads, 2, HEAD_DIM), v_s.reshape(bsz, dec, da_heads, 2 * HEAD_DIM))
    return yp, ys, kv_p + (sp,), kv_s + (ss,)


def kernel(x_prompt, x_sample, cache_k, cache_v, state_hgrn, page_table, norm_pre, norm_post, w_in,
           hg_lower, hg_norm, da_lambda, da_subln, rel_bias, w_up_hg, w_up_da, w_out):
    depth = w_in.shape[0]
    hp, hs = x_prompt, x_sample
    outs = [[] for _ in range(6)]
    for l in range(depth):
        lam_init = 0.8 - 0.6 * math.exp(-0.3 * l)
        w = dict(g_pre=norm_pre[l], g_post=norm_post[l], w_in=w_in[l], hg_lower=hg_lower,
                 hg_norm=hg_norm[l], da_lambda=da_lambda[l], da_subln=da_subln[l], rel_bias=rel_bias,
                 w_up_hg=w_up_hg[l], w_up_da=w_up_da[l], w_out=w_out[l].astype(BF16))
        hp, hs, new_p, new_s = _layer(hp, hs, state_hgrn[l], cache_k[l], cache_v[l], page_table, w,
                                      layer=l, lam_init=lam_init)
        for lst, val in zip(outs, new_p + new_s):
            lst.append(val)
    return (hp, hs) + tuple(jnp.stack(o) for o in outs)
```

```python
import functools
import math

import jax
import jax.numpy as jnp
from jax import lax
from jax.experimental import pallas as pl
from jax.experimental.pallas import tpu as pltpu

HEAD_DIM = 128
REL_BUCKETS = 32
REL_MAX_DIST = 128
EPS = 1e-6
NEG = -1e30
LOG2E = math.log2(math.e)
VMEM_LIMIT = 56 * 1024 * 1024
BF16 = jnp.bfloat16
F32 = jnp.float32

_NT = (((1,), (1,)), ((), ()))
_TN = (((0,), (0,)), ((), ()))


def _params(*sem):
    return pltpu.CompilerParams(dimension_semantics=sem, vmem_limit_bytes=VMEM_LIMIT)


def _silu(z):
    return z * jax.nn.sigmoid(z)


ROW_TILE = 1024


def _tile(n, pref):
    t = min(n, pref)
    assert n % t == 0, (n, t)
    return t


def _rmsnorm_kernel(x_ref, g_ref, o_ref):
    x = x_ref[...]
    ms = jnp.mean(x * x, axis=-1, keepdims=True)
    o_ref[...] = (x * lax.rsqrt(ms + EPS) * g_ref[...]).astype(o_ref.dtype)


def _rmsnorm_cast(x, g):
    m, d = x.shape
    tm = _tile(m, 256)
    return pl.pallas_call(
        _rmsnorm_kernel,
        grid=(m // tm,),
        in_specs=[pl.BlockSpec((tm, d), lambda i: (i, 0)),
                  pl.BlockSpec((1, d), lambda i: (0, 0))],
        out_specs=pl.BlockSpec((tm, d), lambda i: (i, 0)),
        out_shape=jax.ShapeDtypeStruct((m, d), BF16),
        compiler_params=_params("arbitrary"),
        name="rmsnorm_cast",
    )(x, g.reshape(1, d))


def _store_products(acc, out_refs, outs):
    for o_ref, (_, scale, layout) in zip(out_refs, outs):
        val = acc if scale == 1.0 else acc * scale
        o_ref[...] = (val.T if layout == "tiles_t" else val).astype(o_ref.dtype)


def _proj_kernel(*refs, outs, outs2, cast):
    refs = list(refs)
    a_ref = refs.pop(0)
    a2_ref = refs.pop(0) if outs2 else None
    w_ref = refs.pop(0)
    wb_ref = refs.pop() if cast else None
    out_refs, out2_refs = refs[:len(outs)], refs[len(outs):]
    first = pl.program_id(1) == 0

    if cast:
        @pl.when(first)
        def _():
            wb_ref[...] = w_ref[...].astype(BF16)
        wb = wb_ref[...]
    else:
        wb = w_ref[...]

    _store_products(jnp.dot(a_ref[...], wb, preferred_element_type=F32), out_refs, outs)
    if outs2:
        @pl.when(first)
        def _():
            _store_products(jnp.dot(a2_ref[...], wb, preferred_element_type=F32), out2_refs, outs2)


def _proj(a, w, col_start, width, outs, *, tn_pref, name, a2=None, outs2=()):
    m, kd = a.shape
    tm, tn = _tile(m, ROW_TILE), _tile(width, tn_pref)
    assert col_start % tn == 0
    c0 = col_start // tn
    cast = w.dtype != BF16
    outs, outs2 = tuple(outs), tuple(outs2)
    assert all(layout == "rows" for _, _, layout in outs2)

    def out_spec(layout):
        if layout == "tiles_t":
            return pl.BlockSpec((None, tn, tm), lambda j, i: (i, j, 0))
        return pl.BlockSpec((tm, tn), lambda j, i: (i, j))

    def out_shape(dt, layout):
        return jax.ShapeDtypeStruct((m // tm, width, tm) if layout == "tiles_t" else (m, width), dt)

    in_specs = [pl.BlockSpec((tm, kd), lambda j, i: (i, 0))]
    operands = [a]
    if outs2:
        m2 = a2.shape[0]
        in_specs.append(pl.BlockSpec((m2, kd), lambda j, i: (0, 0)))
        operands.append(a2)
    in_specs.append(pl.BlockSpec((kd, tn), lambda j, i: (0, c0 + j)))
    operands.append(w)
    return pl.pallas_call(
        functools.partial(_proj_kernel, outs=outs, outs2=outs2, cast=cast),
        grid=(width // tn, m // tm),
        in_specs=in_specs,
        out_specs=([out_spec(layout) for _, _, layout in outs]
                   + [pl.BlockSpec((a2.shape[0], tn), lambda j, i: (0, j)) for _ in outs2]),
        out_shape=([out_shape(dt, layout) for dt, _, layout in outs]
                   + [jax.ShapeDtypeStruct((a2.shape[0], width), dt) for dt, _, _ in outs2]),
        scratch_shapes=[pltpu.VMEM((kd, tn), BF16)] if cast else [],
        compiler_params=_params("arbitrary", "arbitrary"),
        name=name,
    )(*operands)


def _hgrn_masks(c):
    row = lax.broadcasted_iota(jnp.int32, (c, HEAD_DIM), 0)
    ti = lax.broadcasted_iota(jnp.int32, (c, c), 0)
    si = lax.broadcasted_iota(jnp.int32, (c, c), 1)
    scan, levels = [], []
    sh = 1
    while sh < c:
        scan.append((sh, row >= sh))
        sh *= 2
    n, lg = 1, 0
    while n < c:
        u = ti >> lg
        w = si >> lg
        levels.append((n, ((u ^ w) * 2 + (u & 1)) == 3, (row & n) != 0))
        n *= 2
        lg += 1
    return scan, levels


def _hgrn_head(q, fr, v, z, lo, g, st, masks, *, c, layer):
    scan, levels = masks
    e = jnp.exp(lo - jnp.max(lo, axis=0, keepdims=True))
    lb = jnp.sum(e[:layer + 1], axis=0, keepdims=True) / jnp.sum(e, axis=0, keepdims=True)
    f = lb + (1.0 - lb) * jax.nn.sigmoid(fr)
    k = 1.0 - f

    b = jnp.log2(f)
    for sh, keep in scan:
        b = b + jnp.where(keep, pltpu.roll(b, sh, 0), 0.0)
    b_last = b[c - 1:c, :]

    vb = v.astype(BF16)
    o = lax.dot_general((q * jnp.exp2(b)).astype(BF16), st.astype(BF16), _NT,
                        preferred_element_type=F32)

    att = jnp.zeros((c, c), F32)
    bend = b
    for n, pair_mask, upper in levels:
        bstart = pltpu.roll(bend, n, 0)
        qn = (q * jnp.exp2(b - bstart)).astype(BF16)
        kn = (k * jnp.exp2(bend - b)).astype(BF16)
        an = lax.dot_general(qn, kn, _NT, preferred_element_type=F32)
        att = jnp.where(pair_mask, an, att)
        bend = jnp.where(upper, bend, pltpu.roll(bend, c - n, 0))
    diag = jnp.sum(q * k, axis=-1, keepdims=True)
    o = o + jnp.dot(att.astype(BF16), vb, preferred_element_type=F32) + diag * v

    ke = (k * jnp.exp2(b_last - b)).astype(BF16)
    st_new = jnp.exp2(b_last) * st + lax.dot_general(vb, ke, _TN, preferred_element_type=F32)
    on = o * lax.rsqrt(jnp.mean(o * o, axis=-1, keepdims=True) + EPS) * g
    return on * _silu(z), st_new


def _hgrn_kernel(q_ref, f_ref, i_ref, z_ref, lo_ref, g_ref, s0_ref, o_ref, s_ref, st_ref, *,
                 chunk, layer, hb):
    step = pl.program_id(2)
    last = step == pl.num_programs(2) - 1
    masks = _hgrn_masks(chunk)
    for hh in range(hb):
        sl = slice(hh * HEAD_DIM, (hh + 1) * HEAD_DIM)

        @pl.when(step == 0)
        def _():
            st_ref[hh] = s0_ref[0, hh].T

        out, st_new = _hgrn_head(q_ref[:, sl], f_ref[:, sl], i_ref[:, sl], z_ref[:, sl],
                                 lo_ref[:, sl], g_ref[...], st_ref[hh], masks, c=chunk, layer=layer)
        st_ref[hh] = st_new
        o_ref[:, sl] = out.astype(o_ref.dtype)

        @pl.when(last)
        def _():
            s_ref[0, hh] = st_new.T


def _hgrn(hg, *, hg_lower, hg_norm, s0, batch, seq, heads, layer, chunk, hb, out_dtype):
    nc = seq // chunk
    bw = hb * HEAD_DIM
    nhb = heads // hb

    def col(t):
        return pl.BlockSpec((chunk, bw), lambda b, h, n: (b * nc + n, t * nhb + h))

    state_spec = pl.BlockSpec((1, hb, HEAD_DIM, HEAD_DIM), lambda b, h, n: (b, h, 0, 0))
    return pl.pallas_call(
        functools.partial(_hgrn_kernel, chunk=chunk, layer=layer, hb=hb),
        grid=(batch, nhb, nc),
        in_specs=[col(0), col(1), col(2), col(3),
                  pl.BlockSpec((hg_lower.shape[0], bw), lambda b, h, n: (0, h)),
                  pl.BlockSpec((1, HEAD_DIM), lambda b, h, n: (0, 0)),
                  state_spec],
        out_specs=[pl.BlockSpec((chunk, bw), lambda b, h, n: (b * nc + n, h)), state_spec],
        out_shape=[jax.ShapeDtypeStruct((batch * seq, heads * HEAD_DIM), out_dtype),
                   jax.ShapeDtypeStruct((batch, heads, HEAD_DIM, HEAD_DIM), F32)],
        scratch_shapes=[pltpu.VMEM((hb, HEAD_DIM, HEAD_DIM), F32)],
        compiler_params=_params("arbitrary", "arbitrary", "arbitrary"),
        name="hgrn2",
    )(hg, hg, hg, hg, hg_lower, hg_norm.reshape(1, HEAD_DIM), s0)


def _rel_bias_tile(rel, rb_ref, h):
    max_exact = REL_BUCKETS // 2
    n = jnp.maximum(rel, 0)
    nf = jnp.maximum(n, 1).astype(F32)
    large = max_exact + (jnp.log(nf / max_exact) / math.log(REL_MAX_DIST / max_exact)
                         * (REL_BUCKETS - max_exact)).astype(jnp.int32)
    bucket = jnp.where(n < max_exact, n, jnp.minimum(large, REL_BUCKETS - 1))
    far = rb_ref[REL_BUCKETS - 1, h]
    tile = jnp.zeros(rel.shape, F32)
    for bk in range(REL_BUCKETS - 1):
        tile = jnp.where(bucket == bk, (rb_ref[bk, h] - far) * LOG2E, tile)
    return tile


def _lambda(lam_ref, lam_init):
    lv = lam_ref[...]
    return (jnp.exp(jnp.sum(lv[0:1] * lv[1:2], axis=-1, keepdims=True))
            - jnp.exp(jnp.sum(lv[2:3] * lv[3:4], axis=-1, keepdims=True)) + lam_init)


def _diff_out(o1, o2, lam, g, z, lam_init):
    o = o1 - lam * o2
    on = o * lax.rsqrt(jnp.mean(o * o, axis=-1, keepdims=True) + EPS) * g * (1.0 - lam_init)
    return on * _silu(z)


def _attn_prompt_kernel(rb_ref, q_ref, k_ref, vt_ref, z_ref, lam_ref, g_ref, o_ref,
                        m_ref, l_ref, acc_ref, bias_ref, *, tile, lam_init):
    t = tile
    h, i = pl.program_id(0), pl.program_id(1)

    @pl.when(i == 0)
    def _():
        blk = REL_MAX_DIST
        nb = t // blk
        si = lax.broadcasted_iota(jnp.int32, (blk, blk), 0)
        ti = lax.broadcasted_iota(jnp.int32, (blk, blk), 1)
        on_diag = jnp.where(si <= ti, _rel_bias_tile(ti - si, rb_ref, h), NEG)
        next_blk = _rel_bias_tile(ti - si + blk, rb_ref, h)
        zero = jnp.zeros((blk, blk), F32)
        masked = jnp.full((blk, blk), NEG, F32)
        for kb in range(nb):
            for qb in range(nb):
                rows, cols = slice(kb * blk, (kb + 1) * blk), slice(qb * blk, (qb + 1) * blk)
                bias_ref[0, rows, cols] = (on_diag if qb == kb else next_blk if qb == kb + 1
                                           else zero if qb > kb else masked)
                bias_ref[1, rows, cols] = next_blk if (kb == nb - 1 and qb == 0) else zero

    m_ref[...] = jnp.full(m_ref.shape, NEG, F32)
    l_ref[...] = jnp.zeros(l_ref.shape, F32)
    acc_ref[...] = jnp.zeros(acc_ref.shape, F32)

    def update(j, bias, krows=None, lanes=None):
        krows = slice(0, t) if krows is None else krows
        lanes = slice(0, t) if lanes is None else lanes
        k = k_ref[j][krows, :]
        vt = vt_ref[j][:, krows]
        sts = []
        for c in range(2):
            sl = slice(c * HEAD_DIM, (c + 1) * HEAD_DIM)
            st = lax.dot_general(k[:, sl], q_ref[lanes, sl], _NT, preferred_element_type=F32)
            sts.append(st if bias is None else st + bias)
        ps = []
        for c in range(2):
            m_prev = m_ref[c, :, lanes]
            m_new = jnp.maximum(m_prev, jnp.max(sts[c], axis=0, keepdims=True))
            alpha = jnp.exp2(m_prev - m_new)
            p = jnp.exp2(sts[c] - m_new)
            l_ref[c, :, lanes] = alpha * l_ref[c, :, lanes] + jnp.sum(p, axis=0, keepdims=True)
            m_ref[c, :, lanes] = m_new
            ps.append((alpha, p.astype(BF16)))
        for c in range(2):
            alpha, pb = ps[c]
            acc_ref[c, :, lanes] = alpha * acc_ref[c, :, lanes] + jnp.dot(vt, pb, preferred_element_type=F32)

    def update2(j0):
        kk = [k_ref[j0], k_ref[j0 + 1]]
        vts = [vt_ref[j0], vt_ref[j0 + 1]]
        sts = [[lax.dot_general(kk[u][:, c * HEAD_DIM:(c + 1) * HEAD_DIM],
                                q_ref[:, c * HEAD_DIM:(c + 1) * HEAD_DIM], _NT,
                                preferred_element_type=F32) for u in range(2)] for c in range(2)]
        ps = []
        for c in range(2):
            m_prev = m_ref[c]
            m_new = jnp.maximum(m_prev, jnp.maximum(jnp.max(sts[c][0], axis=0, keepdims=True),
                                                    jnp.max(sts[c][1], axis=0, keepdims=True)))
            alpha = jnp.exp2(m_prev - m_new)
            p0 = jnp.exp2(sts[c][0] - m_new)
            p1 = jnp.exp2(sts[c][1] - m_new)
            l_ref[c] = (alpha * l_ref[c] + jnp.sum(p0, axis=0, keepdims=True)
                        + jnp.sum(p1, axis=0, keepdims=True))
            m_ref[c] = m_new
            ps.append((alpha, p0.astype(BF16), p1.astype(BF16)))
        for c in range(2):
            alpha, p0, p1 = ps[c]
            acc_ref[c] = (alpha * acc_ref[c] + jnp.dot(vts[0], p0, preferred_element_type=F32)
                          + jnp.dot(vts[1], p1, preferred_element_type=F32))

    n_far = jnp.maximum(i - 1, 0)

    def far_pair(jj, carry):
        update2(2 * jj)
        return carry

    lax.fori_loop(0, n_far // 2, far_pair, 0)

    @pl.when(n_far % 2 == 1)
    def _():
        update(n_far - 1, None)

    @pl.when(i >= 1)
    def _():
        update(i - 1, bias_ref[1])

    half = t // 2
    update(i, bias_ref[0, :half, :], krows=slice(0, half))
    update(i, bias_ref[0, half:, half:], krows=slice(half, t), lanes=slice(half, t))
    lam = _lambda(lam_ref, lam_init)
    ot = acc_ref[0] / l_ref[0] - lam * (acc_ref[1] / l_ref[1])
    o = ot.T
    on = o * lax.rsqrt(jnp.mean(o * o, axis=-1, keepdims=True) + EPS) * g_ref[...] * (1.0 - lam_init)
    o_ref[...] = (on * _silu(z_ref[...])).astype(o_ref.dtype)


def _attn_prompt(aq, ak, avt, z, z_col, rel_bias, da_lambda, da_subln, *, seq, heads, lam_init):
    hw = 2 * HEAD_DIM
    nq, _, tile = avt.shape
    assert tile % REL_MAX_DIST == 0 and nq * tile == seq
    cz = z_col // hw
    return pl.pallas_call(
        functools.partial(_attn_prompt_kernel, tile=tile, lam_init=lam_init),
        grid=(heads, nq),
        in_specs=[pl.BlockSpec(memory_space=pltpu.SMEM),
                  pl.BlockSpec((tile, hw), lambda h, i: (i, h)),
                  pl.BlockSpec((nq, tile, hw), lambda h, i: (0, 0, h)),
                  pl.BlockSpec((nq, hw, tile), lambda h, i: (0, h, 0)),
                  pl.BlockSpec((tile, hw), lambda h, i: (i, cz + h)),
                  pl.BlockSpec((4, HEAD_DIM), lambda h, i: (0, 0)),
                  pl.BlockSpec((1, hw), lambda h, i: (0, 0))],
        out_specs=pl.BlockSpec((tile, hw), lambda h, i: (i, h)),
        out_shape=jax.ShapeDtypeStruct((seq, heads * hw), BF16),
        scratch_shapes=[pltpu.VMEM((2, 1, tile), F32), pltpu.VMEM((2, 1, tile), F32),
                        pltpu.VMEM((2, hw, tile), F32), pltpu.VMEM((2, tile, tile), F32)],
        compiler_params=_params("arbitrary", "arbitrary"),
        name="attn_prompt",
    )(rel_bias, aq, ak.reshape(nq, tile, heads * hw), avt, z, da_lambda, da_subln.reshape(1, hw))


PAGES_PER_STEP = 8


def _attn_decode_kernel(pt_ref, rb_ref, q_ref, kn_ref, vn_ref, z_ref, *rest,
                        heads, page, n_pages, dec, group, lam_init):
    del pt_ref
    kc_refs, vc_refs = rest[:group], rest[group:2 * group]
    lam_ref, g_ref, o_ref, qx_ref, m_ref, l_ref, acc_ref, bias_ref, nbias_ref = rest[2 * group:]
    b, p = pl.program_id(0), pl.program_id(1)
    hw = 2 * HEAD_DIM
    r = heads * dec
    n = heads * page
    n_groups = n_pages // group
    past = n_pages * page

    def near_bias(rel):
        return jnp.concatenate([_rel_bias_tile(rel[h * dec:(h + 1) * dec], rb_ref, h)
                                for h in range(heads)], axis=0)

    @pl.when((b == 0) & (p == 0))
    def _():
        row = lax.broadcasted_iota(jnp.int32, (r, n), 0)
        col = lax.broadcasted_iota(jnp.int32, (r, n), 1)
        valid = (col % heads) == (row // dec)
        bias_ref[0] = jnp.where(valid, 0.0, NEG)
        rel = (past + row % dec) - ((n_pages - 1) * page + col // heads)
        bias_ref[1] = jnp.where(valid, near_bias(rel), NEG)
        row = lax.broadcasted_iota(jnp.int32, (r, r), 0)
        col = lax.broadcasted_iota(jnp.int32, (r, r), 1)
        rel = row % dec - col % dec
        valid = jnp.where((col // dec) == (row // dec), rel, -1) >= 0
        nbias_ref[...] = jnp.where(valid, near_bias(rel), NEG)

    @pl.when(p == 0)
    def _():
        m_ref[...] = jnp.full(m_ref.shape, NEG, F32)
        l_ref[...] = jnp.zeros(l_ref.shape, F32)
        acc_ref[...] = jnp.zeros(acc_ref.shape, F32)
        for c in range(2):
            qx_ref[c] = jnp.concatenate(
                [q_ref[:, (2 * h + c) * HEAD_DIM:(2 * h + c + 1) * HEAD_DIM] for h in range(heads)],
                axis=0).astype(BF16)

    def attend(tiles):
        scores = [[lax.dot_general(qx_ref[c], keys_of(c).astype(BF16), _NT,
                                   preferred_element_type=F32) + bias for keys_of, _, bias in tiles]
                  for c in range(2)]
        ps, alphas = [], []
        for c in range(2):
            rows = slice(c * r, (c + 1) * r)
            m_prev = m_ref[rows]
            m_new = m_prev
            for s in scores[c]:
                m_new = jnp.maximum(m_new, jnp.max(s, axis=-1, keepdims=True))
            alpha = jnp.exp2(m_prev - m_new)
            prs = [jnp.exp2(s - m_new) for s in scores[c]]
            rowsum = jnp.sum(prs[0], axis=-1, keepdims=True)
            for pr in prs[1:]:
                rowsum = rowsum + jnp.sum(pr, axis=-1, keepdims=True)
            l_ref[rows] = alpha * l_ref[rows] + rowsum
            m_ref[rows] = m_new
            ps.append([pr.astype(BF16) for pr in prs])
            alphas.append(alpha)
        pv = None
        for ti, (_, vb, _) in enumerate(tiles):
            part = jnp.dot(jnp.concatenate([ps[0][ti], ps[1][ti]], axis=0), vb,
                           preferred_element_type=F32)
            pv = part if pv is None else pv + part
        acc_ref[...] = jnp.concatenate(alphas, axis=0) * acc_ref[...] + pv

    def page_tiles(last_bias):
        tiles = []
        for gi in range(group):
            kc_ref = kc_refs[gi]
            bias = bias_ref[last_bias if gi == group - 1 else 0]
            tiles.append((lambda c, kc_ref=kc_ref: kc_ref[pl.ds(c, n, stride=2), :],
                          vc_refs[gi][...].astype(BF16), bias))
        return tiles

    @pl.when(p < n_groups - 1)
    def _():
        attend(page_tiles(0))

    @pl.when(p == n_groups - 1)
    def _():
        attend(page_tiles(1))

    @pl.when(p == n_groups)
    def _():
        def new_keys(c):
            return jnp.concatenate(
                [kn_ref[:, (2 * h + c) * HEAD_DIM:(2 * h + c + 1) * HEAD_DIM] for h in range(heads)],
                axis=0)
        vnew = jnp.concatenate([vn_ref[:, h * hw:(h + 1) * hw] for h in range(heads)], axis=0)
        attend([(new_keys, vnew.astype(BF16), nbias_ref[...])])
        lam = _lambda(lam_ref, lam_init)
        acc = acc_ref[...] / l_ref[...]
        for h in range(heads):
            sl = slice(h * hw, (h + 1) * hw)
            o_ref[:, sl] = _diff_out(acc[h * dec:(h + 1) * dec], acc[r + h * dec:r + (h + 1) * dec],
                                     lam, g_ref[...], z_ref[:, sl], lam_init).astype(o_ref.dtype)


def _attn_decode(aq, ak, av, z, z_col, cache_k, cache_v, page_table, rel_bias, da_lambda, da_subln, *,
                 batch, dec, heads, lam_init):
    hw = 2 * HEAD_DIM
    w = heads * hw
    n_pool, page = cache_k.shape[0], cache_k.shape[1]
    n_pages = page_table.shape[1]
    group = math.gcd(n_pages, PAGES_PER_STEP)
    n_groups = n_pages // group
    assert page >= REL_MAX_DIST and dec <= REL_MAX_DIST
    kc = cache_k.reshape(n_pool, page * heads * 2, HEAD_DIM)
    vc = cache_v.reshape(n_pool, page * heads, hw)
    r = heads * dec

    def cache_spec(rows, width, gi):
        return pl.BlockSpec(
            (None, rows, width),
            lambda b, p, pt: (pt[b, jnp.minimum(p, n_groups - 1) * group + gi], 0, 0))

    def row_spec(cb):
        return pl.BlockSpec((dec, w), lambda b, p, pt: (b, cb))

    grid_spec = pltpu.PrefetchScalarGridSpec(
        num_scalar_prefetch=1,
        grid=(batch, n_groups + 1),
        in_specs=([pl.BlockSpec(memory_space=pltpu.SMEM),
                   row_spec(0), row_spec(0), row_spec(0), row_spec(z_col // w)]
                  + [cache_spec(page * heads * 2, HEAD_DIM, gi) for gi in range(group)]
                  + [cache_spec(page * heads, hw, gi) for gi in range(group)]
                  + [pl.BlockSpec((4, HEAD_DIM), lambda b, p, pt: (0, 0)),
                     pl.BlockSpec((1, hw), lambda b, p, pt: (0, 0))]),
        out_specs=pl.BlockSpec((dec, w), lambda b, p, pt: (b, 0)),
        scratch_shapes=[pltpu.VMEM((2, r, HEAD_DIM), BF16),
                        pltpu.VMEM((2 * r, 1), F32), pltpu.VMEM((2 * r, 1), F32),
                        pltpu.VMEM((2 * r, hw), F32),
                        pltpu.VMEM((2, r, heads * page), F32), pltpu.VMEM((r, r), F32)],
    )
    return pl.pallas_call(
        functools.partial(_attn_decode_kernel, heads=heads, page=page, n_pages=n_pages, dec=dec,
                          group=group, lam_init=lam_init),
        grid_spec=grid_spec,
        out_shape=jax.ShapeDtypeStruct((batch * dec, w), F32),
        compiler_params=_params("arbitrary", "arbitrary"),
        name="attn_decode",
    )(page_table, rel_bias, aq, ak, av, z, *([kc] * group), *([vc] * group),
      da_lambda, da_subln.reshape(1, hw))


def _up_kernel(oh_ref, oa_ref, wh_ref, wa_ref, gh_ref, ga_ref, o_ref, whb_ref, wab_ref):
    @pl.when(pl.program_id(1) == 0)
    def _():
        whb_ref[...] = wh_ref[...].astype(BF16)
        wab_ref[...] = wa_ref[...].astype(BF16)

    uh = jnp.dot(oh_ref[...].astype(BF16), whb_ref[...], preferred_element_type=F32)
    ua = jnp.dot(oa_ref[...].astype(BF16), wab_ref[...], preferred_element_type=F32)
    o_ref[...] = (jax.nn.sigmoid(gh_ref[...]) * uh + jax.nn.sigmoid(ga_ref[...]) * ua).astype(o_ref.dtype)


def _up_merge(o_h, o_a, w_h, w_a, gates, gh_col, ga_col):
    m, kh = o_h.shape
    ka = o_a.shape[1]
    d = w_h.shape[1]
    tm, tn = _tile(m, ROW_TILE), _tile(d, 512)
    cgh, cga = gh_col // tn, ga_col // tn
    return pl.pallas_call(
        _up_kernel,
        grid=(d // tn, m // tm),
        in_specs=[pl.BlockSpec((tm, kh), lambda j, i: (i, 0)),
                  pl.BlockSpec((tm, ka), lambda j, i: (i, 0)),
                  pl.BlockSpec((kh, tn), lambda j, i: (0, j)),
                  pl.BlockSpec((ka, tn), lambda j, i: (0, j)),
                  pl.BlockSpec((tm, tn), lambda j, i: (i, cgh + j)),
                  pl.BlockSpec((tm, tn), lambda j, i: (i, cga + j))],
        out_specs=pl.BlockSpec((tm, tn), lambda j, i: (i, j)),
        out_shape=jax.ShapeDtypeStruct((m, d), BF16),
        scratch_shapes=[pltpu.VMEM((kh, tn), BF16), pltpu.VMEM((ka, tn), BF16)],
        compiler_params=_params("arbitrary", "arbitrary"),
        name="up_merge",
    )(o_h, o_a, w_h, w_a, gates, gates)


def _out_kernel(a_ref, w_ref, x_ref, g_ref, y_ref):
    out = jnp.dot(a_ref[...], w_ref[...], preferred_element_type=F32)
    ms = jnp.mean(out * out, axis=-1, keepdims=True)
    y_ref[...] = x_ref[...] + out * lax.rsqrt(ms + EPS) * g_ref[...]


def _out_proj(merged, w_out, x, g_post):
    m, d = x.shape
    tm = _tile(m, 128)
    return pl.pallas_call(
        _out_kernel,
        grid=(m // tm,),
        in_specs=[pl.BlockSpec((tm, d), lambda i: (i, 0)),
                  pl.BlockSpec((d, d), lambda i: (0, 0), pipeline_mode=pl.Buffered(1)),
                  pl.BlockSpec((tm, d), lambda i: (i, 0)),
                  pl.BlockSpec((1, d), lambda i: (0, 0))],
        out_specs=pl.BlockSpec((tm, d), lambda i: (i, 0)),
        out_shape=jax.ShapeDtypeStruct((m, d), F32),
        compiler_params=_params("arbitrary"),
        name="out_proj",
    )(merged, w_out, x, g_post.reshape(1, d))


def _mix_and_project(x2, o_h, o_a, gates, w, da_w):
    d = x2.shape[1]
    merged = _up_merge(o_h, o_a, w["w_up_hg"], w["w_up_da"], gates, da_w, da_w + d)
    return _out_proj(merged, w["w_out"], x2, w["g_post"])


def _layer(xp, xs, s0s, cache_k, cache_v, page_table, w, *, layer, lam_init):
    _, seq, d = xp.shape
    bsz, dec, _ = xs.shape
    assert xp.shape[0] == 1
    hg_heads = (d // 2) // HEAD_DIM
    da_heads = (d // 2) // (2 * HEAD_DIM)
    hg_w = hg_heads * HEAD_DIM
    da_w = da_heads * 2 * HEAD_DIM
    col_aq = 4 * hg_w
    qscale = LOG2E * HEAD_DIM ** -0.5
    f32 = (F32, 1.0, "rows")

    xp2, xs2 = xp.reshape(seq, d), xs.reshape(bsz * dec, d)
    xnp, xns = _rmsnorm_cast(xp2, w["g_pre"]), _rmsnorm_cast(xs2, w["g_pre"])
    proj = functools.partial(_proj, xnp, w["w_in"], tn_pref=512, name="in_proj", a2=xns)
    hg_p, hg_s = proj(0, 4 * hg_w, [f32], outs2=[f32])
    aq_p, aq_s = proj(col_aq, da_w, [(BF16, qscale, "rows")], outs2=[(F32, qscale, "rows")])
    k_p, k16_p, k_s = proj(col_aq + da_w, da_w, [f32, (BF16, 1.0, "rows")], outs2=[f32])
    v_p, vt16_p, v_s = proj(col_aq + 2 * da_w, da_w, [f32, (BF16, 1.0, "tiles_t")], outs2=[f32])
    gates_p, gates_s = proj(col_aq + 3 * da_w, da_w + 2 * d, [f32], outs2=[f32])

    s0p = jnp.zeros((1, hg_heads, HEAD_DIM, HEAD_DIM), F32)
    hgrn = functools.partial(_hgrn, hg_lower=w["hg_lower"], hg_norm=w["hg_norm"], heads=hg_heads, layer=layer)
    oh_p, sp = hgrn(hg_p, s0=s0p, batch=1, seq=seq, chunk=math.gcd(seq, 256), hb=math.gcd(hg_heads, 16),
                    out_dtype=BF16)
    oh_s, ss = hgrn(hg_s, s0=s0s, batch=bsz, seq=dec, chunk=math.gcd(dec, 64), hb=math.gcd(hg_heads, 16),
                    out_dtype=F32)

    attn_w = (w["rel_bias"], w["da_lambda"], w["da_subln"])
    oa_p = _attn_prompt(aq_p, k16_p, vt16_p, gates_p, 0, *attn_w, seq=seq, heads=da_heads, lam_init=lam_init)
    oa_s = _attn_decode(aq_s, k_s, v_s, gates_s, 0, cache_k, cache_v, page_table, *attn_w, batch=bsz,
                        dec=dec, heads=da_heads, lam_init=lam_init)

    yp = _mix_and_project(xp2, oh_p, oa_p, gates_p, w, da_w).reshape(xp.shape)
    ys = _mix_and_project(xs2, oh_s, oa_s, gates_s, w, da_w).reshape(xs.shape)
    kv_p = (k_p.reshape(1, seq, da_heads, 2, HEAD_DIM), v_p.reshape(1, seq, da_heads, 2 * HEAD_DIM))
    kv_s = (k_s.reshape(bsz, dec, da_heads, 2, HEAD_DIM), v_s.reshape(bsz, dec, da_heads, 2 * HEAD_DIM))
    return yp, ys, kv_p + (sp,), kv_s + (ss,)


def kernel(x_prompt, x_sample, cache_k, cache_v, state_hgrn, page_table, norm_pre, norm_post, w_in,
           hg_lower, hg_norm, da_lambda, da_subln, rel_bias, w_up_hg, w_up_da, w_out):
    depth = w_in.shape[0]
    hp, hs = x_prompt, x_sample
    outs = [[] for _ in range(6)]
    for l in range(depth):
        lam_init = 0.8 - 0.6 * math.exp(-0.3 * l)
        w = dict(g_pre=norm_pre[l], g_post=norm_post[l], w_in=w_in[l], hg_lower=hg_lower,
                 hg_norm=hg_norm[l], da_lambda=da_lambda[l], da_subln=da_subln[l], rel_bias=rel_bias,
                 w_up_hg=w_up_hg[l], w_up_da=w_up_da[l], w_out=w_out[l].astype(BF16))
        hp, hs, new_p, new_s = _layer(hp, hs, state_hgrn[l], cache_k[l], cache_v[l], page_table, w,
                                      layer=l, lam_init=lam_init)
        for lst, val in zip(outs, new_p + new_s):
            lst.append(val)
    return (hp, hs) + tuple(jnp.stack(o) for o in outs)
```

```python
import functools
import math

import jax
import jax.numpy as jnp
from jax import lax
from jax.experimental import pallas as pl
from jax.experimental.pallas import tpu as pltpu

HEAD_DIM = 128
REL_BUCKETS = 32
REL_MAX_DIST = 128
EPS = 1e-6
NEG = -1e30
LOG2E = math.log2(math.e)
VMEM_LIMIT = 56 * 1024 * 1024
BF16 = jnp.bfloat16
F32 = jnp.float32

_NT = (((1,), (1,)), ((), ()))
_TN = (((0,), (0,)), ((), ()))


def _params(*sem):
    return pltpu.CompilerParams(dimension_semantics=sem, vmem_limit_bytes=VMEM_LIMIT)


def _silu(z):
    return z * jax.nn.sigmoid(z)


ROW_TILE = 1024


def _tile(n, pref):
    t = min(n, pref)
    assert n % t == 0, (n, t)
    return t


def _rmsnorm_kernel(x_ref, g_ref, o_ref):
    x = x_ref[...]
    ms = jnp.mean(x * x, axis=-1, keepdims=True)
    o_ref[...] = (x * lax.rsqrt(ms + EPS) * g_ref[...]).astype(o_ref.dtype)


def _rmsnorm_cast(x, g):
    m, d = x.shape
    tm = _tile(m, 256)
    return pl.pallas_call(
        _rmsnorm_kernel,
        grid=(m // tm,),
        in_specs=[pl.BlockSpec((tm, d), lambda i: (i, 0)),
                  pl.BlockSpec((1, d), lambda i: (0, 0))],
        out_specs=pl.BlockSpec((tm, d), lambda i: (i, 0)),
        out_shape=jax.ShapeDtypeStruct((m, d), BF16),
        compiler_params=_params("arbitrary"),
        name="rmsnorm_cast",
    )(x, g.reshape(1, d))


def _store_products(acc, out_refs, outs):
    for o_ref, (_, scale, layout) in zip(out_refs, outs):
        val = acc if scale == 1.0 else acc * scale
        o_ref[...] = (val.T if layout == "tiles_t" else val).astype(o_ref.dtype)


def _proj_kernel(*refs, outs, outs2, cast):
    refs = list(refs)
    a_ref = refs.pop(0)
    a2_ref = refs.pop(0) if outs2 else None
    w_ref = refs.pop(0)
    wb_ref = refs.pop() if cast else None
    out_refs, out2_refs = refs[:len(outs)], refs[len(outs):]
    first = pl.program_id(1) == 0

    if cast:
        @pl.when(first)
        def _():
            wb_ref[...] = w_ref[...].astype(BF16)
        wb = wb_ref[...]
    else:
        wb = w_ref[...]

    _store_products(jnp.dot(a_ref[...], wb, preferred_element_type=F32), out_refs, outs)
    if outs2:
        @pl.when(first)
        def _():
            _store_products(jnp.dot(a2_ref[...], wb, preferred_element_type=F32), out2_refs, outs2)


def _proj(a, w, col_start, width, outs, *, tn_pref, name, a2=None, outs2=()):
    m, kd = a.shape
    tm, tn = _tile(m, ROW_TILE), _tile(width, tn_pref)
    assert col_start % tn == 0
    c0 = col_start // tn
    cast = w.dtype != BF16
    outs, outs2 = tuple(outs), tuple(outs2)
    assert all(layout == "rows" for _, _, layout in outs2)

    def out_spec(layout):
        if layout == "tiles_t":
            return pl.BlockSpec((None, tn, tm), lambda j, i: (i, j, 0))
        return pl.BlockSpec((tm, tn), lambda j, i: (i, j))

    def out_shape(dt, layout):
        return jax.ShapeDtypeStruct((m // tm, width, tm) if layout == "tiles_t" else (m, width), dt)

    in_specs = [pl.BlockSpec((tm, kd), lambda j, i: (i, 0))]
    operands = [a]
    if outs2:
        m2 = a2.shape[0]
        in_specs.append(pl.BlockSpec((m2, kd), lambda j, i: (0, 0)))
        operands.append(a2)
    in_specs.append(pl.BlockSpec((kd, tn), lambda j, i: (0, c0 + j)))
    operands.append(w)
    return pl.pallas_call(
        functools.partial(_proj_kernel, outs=outs, outs2=outs2, cast=cast),
        grid=(width // tn, m // tm),
        in_specs=in_specs,
        out_specs=([out_spec(layout) for _, _, layout in outs]
                   + [pl.BlockSpec((a2.shape[0], tn), lambda j, i: (0, j)) for _ in outs2]),
        out_shape=([out_shape(dt, layout) for dt, _, layout in outs]
                   + [jax.ShapeDtypeStruct((a2.shape[0], width), dt) for dt, _, _ in outs2]),
        scratch_shapes=[pltpu.VMEM((kd, tn), BF16)] if cast else [],
        compiler_params=_params("arbitrary", "arbitrary"),
        name=name,
    )(*operands)


def _kproj_kernel(a_ref, a2_ref, w_ref, k32_ref, k16_ref, k2_ref, wb_ref):
    first = pl.program_id(1) == 0

    @pl.when(first)
    def _():
        wb_ref[...] = w_ref[...].astype(BF16)
    wb = wb_ref[...]
    acc = jnp.dot(a_ref[...], wb, preferred_element_type=F32)
    k16_ref[...] = acc.astype(BF16)
    for hc in range(k32_ref.shape[1]):
        k32_ref[:, hc, :] = acc[:, hc * HEAD_DIM:(hc + 1) * HEAD_DIM]

    @pl.when(first)
    def _():
        k2_ref[...] = jnp.dot(a2_ref[...], wb, preferred_element_type=F32)


def _kproj(a, a2, w, col_start, width):
    m, kd = a.shape
    m2 = a2.shape[0]
    tm, tn = _tile(m, 512), _tile(width, 1024)
    assert col_start % tn == 0
    c0 = col_start // tn
    return pl.pallas_call(
        _kproj_kernel,
        grid=(width // tn, m // tm),
        in_specs=[pl.BlockSpec((tm, kd), lambda j, i: (i, 0)),
                  pl.BlockSpec((m2, kd), lambda j, i: (0, 0)),
                  pl.BlockSpec((kd, tn), lambda j, i: (0, c0 + j), pipeline_mode=pl.Buffered(1))],
        out_specs=[pl.BlockSpec((tm, tn // HEAD_DIM, HEAD_DIM), lambda j, i: (i, j, 0)),
                   pl.BlockSpec((tm, tn), lambda j, i: (i, j)),
                   pl.BlockSpec((m2, tn), lambda j, i: (0, j))],
        out_shape=[jax.ShapeDtypeStruct((m, width // HEAD_DIM, HEAD_DIM), F32),
                   jax.ShapeDtypeStruct((m, width), BF16),
                   jax.ShapeDtypeStruct((m2, width), F32)],
        scratch_shapes=[pltpu.VMEM((kd, tn), BF16)],
        compiler_params=_params("arbitrary", "arbitrary"),
        name="k_proj",
    )(a, a2, w)


def _hgrn_masks(c):
    row = lax.broadcasted_iota(jnp.int32, (c, HEAD_DIM), 0)
    ti = lax.broadcasted_iota(jnp.int32, (c, c), 0)
    si = lax.broadcasted_iota(jnp.int32, (c, c), 1)
    scan, levels = [], []
    sh = 1
    while sh < c:
        scan.append((sh, row >= sh))
        sh *= 2
    n, lg = 1, 0
    while n < c:
        u = ti >> lg
        w = si >> lg
        levels.append((n, ((u ^ w) * 2 + (u & 1)) == 3, (row & n) != 0))
        n *= 2
        lg += 1
    return scan, levels


def _hgrn_head(q, fr, v, z, lo, g, st, masks, *, c, layer):
    scan, levels = masks
    e = jnp.exp(lo - jnp.max(lo, axis=0, keepdims=True))
    lb = jnp.sum(e[:layer + 1], axis=0, keepdims=True) / jnp.sum(e, axis=0, keepdims=True)
    f = lb + (1.0 - lb) * jax.nn.sigmoid(fr)
    k = 1.0 - f

    b = jnp.log2(f)
    for sh, keep in scan:
        b = b + jnp.where(keep, pltpu.roll(b, sh, 0), 0.0)
    b_last = b[c - 1:c, :]

    vb = v.astype(BF16)
    o = lax.dot_general((q * jnp.exp2(b)).astype(BF16), st.astype(BF16), _NT,
                        preferred_element_type=F32)

    att = jnp.zeros((c, c), F32)
    bend = b
    for n, pair_mask, upper in levels:
        bstart = pltpu.roll(bend, n, 0)
        qn = (q * jnp.exp2(b - bstart)).astype(BF16)
        kn = (k * jnp.exp2(bend - b)).astype(BF16)
        an = lax.dot_general(qn, kn, _NT, preferred_element_type=F32)
        att = jnp.where(pair_mask, an, att)
        bend = jnp.where(upper, bend, pltpu.roll(bend, c - n, 0))
    diag = jnp.sum(q * k, axis=-1, keepdims=True)
    o = o + jnp.dot(att.astype(BF16), vb, preferred_element_type=F32) + diag * v

    ke = (k * jnp.exp2(b_last - b)).astype(BF16)
    st_new = jnp.exp2(b_last) * st + lax.dot_general(vb, ke, _TN, preferred_element_type=F32)
    on = o * lax.rsqrt(jnp.mean(o * o, axis=-1, keepdims=True) + EPS) * g
    return on * _silu(z), st_new


def _hgrn_kernel(q_ref, f_ref, i_ref, z_ref, lo_ref, g_ref, s0_ref, o_ref, s_ref, st_ref, *,
                 chunk, layer, hb):
    step = pl.program_id(2)
    last = step == pl.num_programs(2) - 1
    masks = _hgrn_masks(chunk)
    for hh in range(hb):
        sl = slice(hh * HEAD_DIM, (hh + 1) * HEAD_DIM)

        @pl.when(step == 0)
        def _():
            st_ref[hh] = s0_ref[0, hh].T

        out, st_new = _hgrn_head(q_ref[:, sl], f_ref[:, sl], i_ref[:, sl], z_ref[:, sl],
                                 lo_ref[:, sl], g_ref[...], st_ref[hh], masks, c=chunk, layer=layer)
        st_ref[hh] = st_new
        o_ref[:, sl] = out.astype(o_ref.dtype)

        @pl.when(last)
        def _():
            s_ref[0, hh] = st_new.T


def _hgrn(hg, *, hg_lower, hg_norm, s0, batch, seq, heads, layer, chunk, hb, out_dtype):
    nc = seq // chunk
    bw = hb * HEAD_DIM
    nhb = heads // hb

    def col(t):
        return pl.BlockSpec((chunk, bw), lambda b, h, n: (b * nc + n, t * nhb + h))

    state_spec = pl.BlockSpec((1, hb, HEAD_DIM, HEAD_DIM), lambda b, h, n: (b, h, 0, 0))
    return pl.pallas_call(
        functools.partial(_hgrn_kernel, chunk=chunk, layer=layer, hb=hb),
        grid=(batch, nhb, nc),
        in_specs=[col(0), col(1), col(2), col(3),
                  pl.BlockSpec((hg_lower.shape[0], bw), lambda b, h, n: (0, h)),
                  pl.BlockSpec((1, HEAD_DIM), lambda b, h, n: (0, 0)),
                  state_spec],
        out_specs=[pl.BlockSpec((chunk, bw), lambda b, h, n: (b * nc + n, h)), state_spec],
        out_shape=[jax.ShapeDtypeStruct((batch * seq, heads * HEAD_DIM), out_dtype),
                   jax.ShapeDtypeStruct((batch, heads, HEAD_DIM, HEAD_DIM), F32)],
        scratch_shapes=[pltpu.VMEM((hb, HEAD_DIM, HEAD_DIM), F32)],
        compiler_params=_params("arbitrary", "arbitrary", "arbitrary"),
        name="hgrn2",
    )(hg, hg, hg, hg, hg_lower, hg_norm.reshape(1, HEAD_DIM), s0)


def _rel_bias_tile(rel, rb_ref, h):
    max_exact = REL_BUCKETS // 2
    n = jnp.maximum(rel, 0)
    nf = jnp.maximum(n, 1).astype(F32)
    large = max_exact + (jnp.log(nf / max_exact) / math.log(REL_MAX_DIST / max_exact)
                         * (REL_BUCKETS - max_exact)).astype(jnp.int32)
    bucket = jnp.where(n < max_exact, n, jnp.minimum(large, REL_BUCKETS - 1))
    far = rb_ref[REL_BUCKETS - 1, h]
    tile = jnp.zeros(rel.shape, F32)
    for bk in range(REL_BUCKETS - 1):
        tile = jnp.where(bucket == bk, (rb_ref[bk, h] - far) * LOG2E, tile)
    return tile


def _lambda(lam_ref, lam_init):
    lv = lam_ref[...]
    return (jnp.exp(jnp.sum(lv[0:1] * lv[1:2], axis=-1, keepdims=True))
            - jnp.exp(jnp.sum(lv[2:3] * lv[3:4], axis=-1, keepdims=True)) + lam_init)


def _diff_out(o1, o2, lam, g, z, lam_init):
    o = o1 - lam * o2
    on = o * lax.rsqrt(jnp.mean(o * o, axis=-1, keepdims=True) + EPS) * g * (1.0 - lam_init)
    return on * _silu(z)


def _attn_prompt_kernel(rb_ref, q_ref, k_ref, vt_ref, z_ref, lam_ref, g_ref, o_ref,
                        m_ref, l_ref, acc_ref, bias_ref, *, tile, lam_init):
    t = tile
    h, i = pl.program_id(0), pl.program_id(1)

    @pl.when(i == 0)
    def _():
        blk = REL_MAX_DIST
        nb = t // blk
        si = lax.broadcasted_iota(jnp.int32, (blk, blk), 0)
        ti = lax.broadcasted_iota(jnp.int32, (blk, blk), 1)
        on_diag = jnp.where(si <= ti, _rel_bias_tile(ti - si, rb_ref, h), NEG)
        next_blk = _rel_bias_tile(ti - si + blk, rb_ref, h)
        zero = jnp.zeros((blk, blk), F32)
        masked = jnp.full((blk, blk), NEG, F32)
        for kb in range(nb):
            for qb in range(nb):
                rows, cols = slice(kb * blk, (kb + 1) * blk), slice(qb * blk, (qb + 1) * blk)
                bias_ref[0, rows, cols] = (on_diag if qb == kb else next_blk if qb == kb + 1
                                           else zero if qb > kb else masked)
                bias_ref[1, rows, cols] = next_blk if (kb == nb - 1 and qb == 0) else zero

    m_ref[...] = jnp.full(m_ref.shape, NEG, F32)
    l_ref[...] = jnp.zeros(l_ref.shape, F32)
    acc_ref[...] = jnp.zeros(acc_ref.shape, F32)

    def update(j, bias, krows=None, lanes=None):
        krows = slice(0, t) if krows is None else krows
        lanes = slice(0, t) if lanes is None else lanes
        k = k_ref[j][krows, :]
        vt = vt_ref[j][:, krows]
        sts = []
        for c in range(2):
            sl = slice(c * HEAD_DIM, (c + 1) * HEAD_DIM)
            st = lax.dot_general(k[:, sl], q_ref[lanes, sl], _NT, preferred_element_type=F32)
            sts.append(st if bias is None else st + bias)
        ps = []
        for c in range(2):
            m_prev = m_ref[c, :, lanes]
            m_new = jnp.maximum(m_prev, jnp.max(sts[c], axis=0, keepdims=True))
            alpha = jnp.exp2(m_prev - m_new)
            p = jnp.exp2(sts[c] - m_new)
            l_ref[c, :, lanes] = alpha * l_ref[c, :, lanes] + jnp.sum(p, axis=0, keepdims=True)
            m_ref[c, :, lanes] = m_new
            ps.append((alpha, p.astype(BF16)))
        for c in range(2):
            alpha, pb = ps[c]
            acc_ref[c, :, lanes] = alpha * acc_ref[c, :, lanes] + jnp.dot(vt, pb, preferred_element_type=F32)

    def update2(j0):
        kk = [k_ref[j0], k_ref[j0 + 1]]
        vts = [vt_ref[j0], vt_ref[j0 + 1]]
        sts = [[lax.dot_general(kk[u][:, c * HEAD_DIM:(c + 1) * HEAD_DIM],
                                q_ref[:, c * HEAD_DIM:(c + 1) * HEAD_DIM], _NT,
                                preferred_element_type=F32) for u in range(2)] for c in range(2)]
        ps = []
        for c in range(2):
            m_prev = m_ref[c]
            m_new = jnp.maximum(m_prev, jnp.maximum(jnp.max(sts[c][0], axis=0, keepdims=True),
                                                    jnp.max(sts[c][1], axis=0, keepdims=True)))
            alpha = jnp.exp2(m_prev - m_new)
            p0 = jnp.exp2(sts[c][0] - m_new)
            p1 = jnp.exp2(sts[c][1] - m_new)
            l_ref[c] = (alpha * l_ref[c] + jnp.sum(p0, axis=0, keepdims=True)
                        + jnp.sum(p1, axis=0, keepdims=True))
            m_ref[c] = m_new
            ps.append((alpha, p0.astype(BF16), p1.astype(BF16)))
        for c in range(2):
            alpha, p0, p1 = ps[c]
            acc_ref[c] = (alpha * acc_ref[c] + jnp.dot(vts[0], p0, preferred_element_type=F32)
                          + jnp.dot(vts[1], p1, preferred_element_type=F32))

    n_far = jnp.maximum(i - 1, 0)

    def far_pair(jj, carry):
        update2(2 * jj)
        return carry

    lax.fori_loop(0, n_far // 2, far_pair, 0)

    @pl.when(n_far % 2 == 1)
    def _():
        update(n_far - 1, None)

    @pl.when(i >= 1)
    def _():
        update(i - 1, bias_ref[1])

    half = t // 2
    update(i, bias_ref[0, :half, :], krows=slice(0, half))
    update(i, bias_ref[0, half:, half:], krows=slice(half, t), lanes=slice(half, t))
    lam = _lambda(lam_ref, lam_init)
    ot = acc_ref[0] / l_ref[0] - lam * (acc_ref[1] / l_ref[1])
    o = ot.T
    on = o * lax.rsqrt(jnp.mean(o * o, axis=-1, keepdims=True) + EPS) * g_ref[...] * (1.0 - lam_init)
    o_ref[...] = (on * _silu(z_ref[...])).astype(o_ref.dtype)


def _attn_prompt(aq, ak, avt, z, z_col, rel_bias, da_lambda, da_subln, *, seq, heads, lam_init):
    hw = 2 * HEAD_DIM
    nq, _, tile = avt.shape
    assert tile % REL_MAX_DIST == 0 and nq * tile == seq
    cz = z_col // hw
    return pl.pallas_call(
        functools.partial(_attn_prompt_kernel, tile=tile, lam_init=lam_init),
        grid=(heads, nq),
        in_specs=[pl.BlockSpec(memory_space=pltpu.SMEM),
                  pl.BlockSpec((tile, hw), lambda h, i: (i, h)),
                  pl.BlockSpec((nq, tile, hw), lambda h, i: (0, 0, h)),
                  pl.BlockSpec((nq, hw, tile), lambda h, i: (0, h, 0)),
                  pl.BlockSpec((tile, hw), lambda h, i: (i, cz + h)),
                  pl.BlockSpec((4, HEAD_DIM), lambda h, i: (0, 0)),
                  pl.BlockSpec((1, hw), lambda h, i: (0, 0))],
        out_specs=pl.BlockSpec((tile, hw), lambda h, i: (i, h)),
        out_shape=jax.ShapeDtypeStruct((seq, heads * hw), BF16),
        scratch_shapes=[pltpu.VMEM((2, 1, tile), F32), pltpu.VMEM((2, 1, tile), F32),
                        pltpu.VMEM((2, hw, tile), F32), pltpu.VMEM((2, tile, tile), F32)],
        compiler_params=_params("arbitrary", "arbitrary"),
        name="attn_prompt",
    )(rel_bias, aq, ak.reshape(nq, tile, heads * hw), avt, z, da_lambda, da_subln.reshape(1, hw))


PAGES_PER_STEP = 8


def _attn_decode_kernel(pt_ref, rb_ref, q_ref, kn_ref, vn_ref, z_ref, *rest,
                        heads, page, n_pages, dec, group, lam_init):
    del pt_ref
    kc_refs, vc_refs = rest[:group], rest[group:2 * group]
    lam_ref, g_ref, o_ref, qx_ref, m_ref, l_ref, acc_ref, bias_ref, nbias_ref = rest[2 * group:]
    b, p = pl.program_id(0), pl.program_id(1)
    hw = 2 * HEAD_DIM
    r = heads * dec
    n = heads * page
    n_groups = n_pages // group
    past = n_pages * page

    def near_bias(rel):
        return jnp.concatenate([_rel_bias_tile(rel[h * dec:(h + 1) * dec], rb_ref, h)
                                for h in range(heads)], axis=0)

    @pl.when((b == 0) & (p == 0))
    def _():
        row = lax.broadcasted_iota(jnp.int32, (r, n), 0)
        col = lax.broadcasted_iota(jnp.int32, (r, n), 1)
        valid = (col % heads) == (row // dec)
        bias_ref[0] = jnp.where(valid, 0.0, NEG)
        rel = (past + row % dec) - ((n_pages - 1) * page + col // heads)
        bias_ref[1] = jnp.where(valid, near_bias(rel), NEG)
        row = lax.broadcasted_iota(jnp.int32, (r, r), 0)
        col = lax.broadcasted_iota(jnp.int32, (r, r), 1)
        rel = row % dec - col % dec
        valid = jnp.where((col // dec) == (row // dec), rel, -1) >= 0
        nbias_ref[...] = jnp.where(valid, near_bias(rel), NEG)

    @pl.when(p == 0)
    def _():
        m_ref[...] = jnp.full(m_ref.shape, NEG, F32)
        l_ref[...] = jnp.zeros(l_ref.shape, F32)
        acc_ref[...] = jnp.zeros(acc_ref.shape, F32)
        for c in range(2):
            qx_ref[c] = jnp.concatenate(
                [q_ref[:, (2 * h + c) * HEAD_DIM:(2 * h + c + 1) * HEAD_DIM] for h in range(heads)],
                axis=0).astype(BF16)

    def attend(tiles):
        scores = [[lax.dot_general(qx_ref[c], keys_of(c).astype(BF16), _NT,
                                   preferred_element_type=F32) + bias for keys_of, _, bias in tiles]
                  for c in range(2)]
        ps, alphas = [], []
        for c in range(2):
            rows = slice(c * r, (c + 1) * r)
            m_prev = m_ref[rows]
            m_new = m_prev
            for s in scores[c]:
                m_new = jnp.maximum(m_new, jnp.max(s, axis=-1, keepdims=True))
            alpha = jnp.exp2(m_prev - m_new)
            prs = [jnp.exp2(s - m_new) for s in scores[c]]
            rowsum = jnp.sum(prs[0], axis=-1, keepdims=True)
            for pr in prs[1:]:
                rowsum = rowsum + jnp.sum(pr, axis=-1, keepdims=True)
            l_ref[rows] = alpha * l_ref[rows] + rowsum
            m_ref[rows] = m_new
            ps.append([pr.astype(BF16) for pr in prs])
            alphas.append(alpha)
        pv = None
        for ti, (_, vb, _) in enumerate(tiles):
            part = jnp.dot(jnp.concatenate([ps[0][ti], ps[1][ti]], axis=0), vb,
                           preferred_element_type=F32)
            pv = part if pv is None else pv + part
        acc_ref[...] = jnp.concatenate(alphas, axis=0) * acc_ref[...] + pv

    def page_tiles(last_bias):
        tiles = []
        for gi in range(group):
            kc_ref = kc_refs[gi]
            bias = bias_ref[last_bias if gi == group - 1 else 0]
            tiles.append((lambda c, kc_ref=kc_ref: kc_ref[pl.ds(c, n, stride=2), :],
                          vc_refs[gi][...].astype(BF16), bias))
        return tiles

    @pl.when(p < n_groups - 1)
    def _():
        attend(page_tiles(0))

    @pl.when(p == n_groups - 1)
    def _():
        attend(page_tiles(1))

    @pl.when(p == n_groups)
    def _():
        def new_keys(c):
            return jnp.concatenate(
                [kn_ref[:, (2 * h + c) * HEAD_DIM:(2 * h + c + 1) * HEAD_DIM] for h in range(heads)],
                axis=0)
        vnew = jnp.concatenate([vn_ref[:, h * hw:(h + 1) * hw] for h in range(heads)], axis=0)
        attend([(new_keys, vnew.astype(BF16), nbias_ref[...])])
        lam = _lambda(lam_ref, lam_init)
        acc = acc_ref[...] / l_ref[...]
        for h in range(heads):
            sl = slice(h * hw, (h + 1) * hw)
            o_ref[:, sl] = _diff_out(acc[h * dec:(h + 1) * dec], acc[r + h * dec:r + (h + 1) * dec],
                                     lam, g_ref[...], z_ref[:, sl], lam_init).astype(o_ref.dtype)


def _attn_decode(aq, ak, av, z, z_col, cache_k, cache_v, page_table, rel_bias, da_lambda, da_subln, *,
                 batch, dec, heads, lam_init):
    hw = 2 * HEAD_DIM
    w = heads * hw
    n_pool, page = cache_k.shape[0], cache_k.shape[1]
    n_pages = page_table.shape[1]
    group = math.gcd(n_pages, PAGES_PER_STEP)
    n_groups = n_pages // group
    assert page >= REL_MAX_DIST and dec <= REL_MAX_DIST
    kc = cache_k.reshape(n_pool, page * heads * 2, HEAD_DIM)
    vc = cache_v.reshape(n_pool, page * heads, hw)
    r = heads * dec

    def cache_spec(rows, width, gi):
        return pl.BlockSpec(
            (None, rows, width),
            lambda b, p, pt: (pt[b, jnp.minimum(p, n_groups - 1) * group + gi], 0, 0))

    def row_spec(cb):
        return pl.BlockSpec((dec, w), lambda b, p, pt: (b, cb))

    grid_spec = pltpu.PrefetchScalarGridSpec(
        num_scalar_prefetch=1,
        grid=(batch, n_groups + 1),
        in_specs=([pl.BlockSpec(memory_space=pltpu.SMEM),
                   row_spec(0), row_spec(0), row_spec(0), row_spec(z_col // w)]
                  + [cache_spec(page * heads * 2, HEAD_DIM, gi) for gi in range(group)]
                  + [cache_spec(page * heads, hw, gi) for gi in range(group)]
                  + [pl.BlockSpec((4, HEAD_DIM), lambda b, p, pt: (0, 0)),
                     pl.BlockSpec((1, hw), lambda b, p, pt: (0, 0))]),
        out_specs=pl.BlockSpec((dec, w), lambda b, p, pt: (b, 0)),
        scratch_shapes=[pltpu.VMEM((2, r, HEAD_DIM), BF16),
                        pltpu.VMEM((2 * r, 1), F32), pltpu.VMEM((2 * r, 1), F32),
                        pltpu.VMEM((2 * r, hw), F32),
                        pltpu.VMEM((2, r, heads * page), F32), pltpu.VMEM((r, r), F32)],
    )
    return pl.pallas_call(
        functools.partial(_attn_decode_kernel, heads=heads, page=page, n_pages=n_pages, dec=dec,
                          group=group, lam_init=lam_init),
        grid_spec=grid_spec,
        out_shape=jax.ShapeDtypeStruct((batch * dec, w), F32),
        compiler_params=_params("arbitrary", "arbitrary"),
        name="attn_decode",
    )(page_table, rel_bias, aq, ak, av, z, *([kc] * group), *([vc] * group),
      da_lambda, da_subln.reshape(1, hw))


def _up_kernel(oh_ref, oa_ref, wh_ref, wa_ref, gh_ref, ga_ref, o_ref, whb_ref, wab_ref):
    @pl.when(pl.program_id(1) == 0)
    def _():
        whb_ref[...] = wh_ref[...].astype(BF16)
        wab_ref[...] = wa_ref[...].astype(BF16)

    uh = jnp.dot(oh_ref[...].astype(BF16), whb_ref[...], preferred_element_type=F32)
    ua = jnp.dot(oa_ref[...].astype(BF16), wab_ref[...], preferred_element_type=F32)
    o_ref[...] = (jax.nn.sigmoid(gh_ref[...]) * uh + jax.nn.sigmoid(ga_ref[...]) * ua).astype(o_ref.dtype)


def _up_merge(o_h, o_a, w_h, w_a, gates, gh_col, ga_col):
    m, kh = o_h.shape
    ka = o_a.shape[1]
    d = w_h.shape[1]
    tm, tn = _tile(m, ROW_TILE), _tile(d, 512)
    cgh, cga = gh_col // tn, ga_col // tn
    return pl.pallas_call(
        _up_kernel,
        grid=(d // tn, m // tm),
        in_specs=[pl.BlockSpec((tm, kh), lambda j, i: (i, 0)),
                  pl.BlockSpec((tm, ka), lambda j, i: (i, 0)),
                  pl.BlockSpec((kh, tn), lambda j, i: (0, j)),
                  pl.BlockSpec((ka, tn), lambda j, i: (0, j)),
                  pl.BlockSpec((tm, tn), lambda j, i: (i, cgh + j)),
                  pl.BlockSpec((tm, tn), lambda j, i: (i, cga + j))],
        out_specs=pl.BlockSpec((tm, tn), lambda j, i: (i, j)),
        out_shape=jax.ShapeDtypeStruct((m, d), BF16),
        scratch_shapes=[pltpu.VMEM((kh, tn), BF16), pltpu.VMEM((ka, tn), BF16)],
        compiler_params=_params("arbitrary", "arbitrary"),
        name="up_merge",
    )(o_h, o_a, w_h, w_a, gates, gates)


def _out_kernel(a_ref, w_ref, x_ref, g_ref, y_ref):
    out = jnp.dot(a_ref[...], w_ref[...], preferred_element_type=F32)
    ms = jnp.mean(out * out, axis=-1, keepdims=True)
    y_ref[...] = x_ref[...] + out * lax.rsqrt(ms + EPS) * g_ref[...]


def _out_proj(merged, w_out, x, g_post):
    m, d = x.shape
    tm = _tile(m, 128)
    return pl.pallas_call(
        _out_kernel,
        grid=(m // tm,),
        in_specs=[pl.BlockSpec((tm, d), lambda i: (i, 0)),
                  pl.BlockSpec((d, d), lambda i: (0, 0), pipeline_mode=pl.Buffered(1)),
                  pl.BlockSpec((tm, d), lambda i: (i, 0)),
                  pl.BlockSpec((1, d), lambda i: (0, 0))],
        out_specs=pl.BlockSpec((tm, d), lambda i: (i, 0)),
        out_shape=jax.ShapeDtypeStruct((m, d), F32),
        compiler_params=_params("arbitrary"),
        name="out_proj",
    )(merged, w_out, x, g_post.reshape(1, d))


def _mix_and_project(x2, o_h, o_a, gates, w, da_w):
    d = x2.shape[1]
    merged = _up_merge(o_h, o_a, w["w_up_hg"], w["w_up_da"], gates, da_w, da_w + d)
    return _out_proj(merged, w["w_out"], x2, w["g_post"])


def _layer(xp, xs, s0s, cache_k, cache_v, page_table, w, *, layer, lam_init):
    _, seq, d = xp.shape
    bsz, dec, _ = xs.shape
    assert xp.shape[0] == 1
    hg_heads = (d // 2) // HEAD_DIM
    da_heads = (d // 2) // (2 * HEAD_DIM)
    hg_w = hg_heads * HEAD_DIM
    da_w = da_heads * 2 * HEAD_DIM
    col_aq = 4 * hg_w
    qscale = LOG2E * HEAD_DIM ** -0.5
    f32 = (F32, 1.0, "rows")

    xp2, xs2 = xp.reshape(seq, d), xs.reshape(bsz * dec, d)
    xnp, xns = _rmsnorm_cast(xp2, w["g_pre"]), _rmsnorm_cast(xs2, w["g_pre"])
    proj = functools.partial(_proj, xnp, w["w_in"], tn_pref=512, name="in_proj", a2=xns)
    hg_p, hg_s = proj(0, 4 * hg_w, [f32], outs2=[f32])
    aq_p, aq_s = proj(col_aq, da_w, [(BF16, qscale, "rows")], outs2=[(F32, qscale, "rows")])
    k_p, k16_p, k_s = _kproj(xnp, xns, w["w_in"], col_aq + da_w, da_w)
    v_p, vt16_p, v_s = proj(col_aq + 2 * da_w, da_w, [f32, (BF16, 1.0, "tiles_t")], outs2=[f32])
    gates_p, gates_s = proj(col_aq + 3 * da_w, da_w + 2 * d, [f32], outs2=[f32])

    s0p = jnp.zeros((1, hg_heads, HEAD_DIM, HEAD_DIM), F32)
    hgrn = functools.partial(_hgrn, hg_lower=w["hg_lower"], hg_norm=w["hg_norm"], heads=hg_heads, layer=layer)
    oh_p, sp = hgrn(hg_p, s0=s0p, batch=1, seq=seq, chunk=math.gcd(seq, 256), hb=math.gcd(hg_heads, 16),
                    out_dtype=BF16)
    oh_s, ss = hgrn(hg_s, s0=s0s, batch=bsz, seq=dec, chunk=math.gcd(dec, 64), hb=math.gcd(hg_heads, 16),
                    out_dtype=F32)

    attn_w = (w["rel_bias"], w["da_lambda"], w["da_subln"])
    oa_p = _attn_prompt(aq_p, k16_p, vt16_p, gates_p, 0, *attn_w, seq=seq, heads=da_heads, lam_init=lam_init)
    oa_s = _attn_decode(aq_s, k_s, v_s, gates_s, 0, cache_k, cache_v, page_table, *attn_w, batch=bsz,
                        dec=dec, heads=da_heads, lam_init=lam_init)

    yp = _mix_and_project(xp2, oh_p, oa_p, gates_p, w, da_w).reshape(xp.shape)
    ys = _mix_and_project(xs2, oh_s, oa_s, gates_s, w, da_w).reshape(xs.shape)
    kv_p = (k_p.reshape(1, seq, da_heads, 2, HEAD_DIM), v_p.reshape(1, seq, da_heads, 2 * HEAD_DIM))
    kv_s = (k_s.reshape(bsz, dec, da_heads, 2, HEAD_DIM), v_s.reshape(bsz, dec, da_heads, 2 * HEAD_DIM))
    return yp, ys, kv_p + (sp,), kv_s + (ss,)


def kernel(x_prompt, x_sample, cache_k, cache_v, state_hgrn, page_table, norm_pre, norm_post, w_in,
           hg_lower, hg_norm, da_lambda, da_subln, rel_bias, w_up_hg, w_up_da, w_out):
    depth = w_in.shape[0]
    hp, hs = x_prompt, x_sample
    outs = [[] for _ in range(6)]
    for l in range(depth):
        lam_init = 0.8 - 0.6 * math.exp(-0.3 * l)
        w = dict(g_pre=norm_pre[l], g_post=norm_post[l], w_in=w_in[l], hg_lower=hg_lower,
                 hg_norm=hg_norm[l], da_lambda=da_lambda[l], da_subln=da_subln[l], rel_bias=rel_bias,
                 w_up_hg=w_up_hg[l], w_up_da=w_up_da[l], w_out=w_out[l].astype(BF16))
        hp, hs, new_p, new_s = _layer(hp, hs, state_hgrn[l], cache_k[l], cache_v[l], page_table, w,
                                      layer=l, lam_init=lam_init)
        for lst, val in zip(outs, new_p + new_s):
            lst.append(val)
    return (hp, hs) + tuple(jnp.stack(o) for o in outs)
```

```python
import functools
import math

import jax
import jax.numpy as jnp
from jax import lax
from jax.experimental import pallas as pl
from jax.experimental.pallas import tpu as pltpu

HEAD_DIM = 128
REL_BUCKETS = 32
REL_MAX_DIST = 128
EPS = 1e-6
NEG = -1e30
LOG2E = math.log2(math.e)
VMEM_LIMIT = 56 * 1024 * 1024
BF16 = jnp.bfloat16
F32 = jnp.float32

ROW_TILE = 1024
PROJ_COLS = 512
KPROJ_ROWS = 512
KPROJ_COLS = 1024
NORM_ROWS = 256
OUT_ROWS = 128
HGRN_CHUNK = 256
HGRN_DEC_CHUNK = 64
HGRN_HEADS = 16
PAGES_PER_STEP = 8

_NT = (((1,), (1,)), ((), ()))
_TN = (((0,), (0,)), ((), ()))


def _params(*sem):
    return pltpu.CompilerParams(dimension_semantics=sem, vmem_limit_bytes=VMEM_LIMIT)


def _silu(z):
    return z * jax.nn.sigmoid(z)


def _tile(n, pref):
    t = min(n, pref)
    assert n % t == 0, (n, t)
    return t


def _rmsnorm_kernel(x_ref, g_ref, o_ref):
    x = x_ref[...]
    ms = jnp.mean(x * x, axis=-1, keepdims=True)
    o_ref[...] = (x * lax.rsqrt(ms + EPS) * g_ref[...]).astype(o_ref.dtype)


def _rmsnorm_cast(x, g):
    m, d = x.shape
    tm = _tile(m, NORM_ROWS)
    return pl.pallas_call(
        _rmsnorm_kernel,
        grid=(m // tm,),
        in_specs=[pl.BlockSpec((tm, d), lambda i: (i, 0)),
                  pl.BlockSpec((1, d), lambda i: (0, 0))],
        out_specs=pl.BlockSpec((tm, d), lambda i: (i, 0)),
        out_shape=jax.ShapeDtypeStruct((m, d), BF16),
        compiler_params=_params("arbitrary"),
        name="rmsnorm_cast",
    )(x, g.reshape(1, d))


def _store_products(acc, out_refs, outs):
    for o_ref, (_, scale, layout) in zip(out_refs, outs):
        val = acc if scale == 1.0 else acc * scale
        o_ref[...] = (val.T if layout == "tiles_t" else val).astype(o_ref.dtype)


def _proj_kernel(*refs, outs, outs2, cast):
    refs = list(refs)
    a_ref = refs.pop(0)
    a2_ref = refs.pop(0) if outs2 else None
    w_ref = refs.pop(0)
    wb_ref = refs.pop() if cast else None
    out_refs, out2_refs = refs[:len(outs)], refs[len(outs):]
    first = pl.program_id(1) == 0

    if cast:
        @pl.when(first)
        def _():
            wb_ref[...] = w_ref[...].astype(BF16)
        wb = wb_ref[...]
    else:
        wb = w_ref[...]

    _store_products(jnp.dot(a_ref[...], wb, preferred_element_type=F32), out_refs, outs)
    if outs2:
        @pl.when(first)
        def _():
            _store_products(jnp.dot(a2_ref[...], wb, preferred_element_type=F32), out2_refs, outs2)


def _proj(a, w, col_start, width, outs, *, tn_pref, name, a2=None, outs2=()):
    m, kd = a.shape
    tm, tn = _tile(m, ROW_TILE), _tile(width, tn_pref)
    assert col_start % tn == 0
    c0 = col_start // tn
    cast = w.dtype != BF16
    outs, outs2 = tuple(outs), tuple(outs2)
    assert all(layout == "rows" for _, _, layout in outs2)

    def out_spec(layout):
        if layout == "tiles_t":
            return pl.BlockSpec((None, tn, tm), lambda j, i: (i, j, 0))
        return pl.BlockSpec((tm, tn), lambda j, i: (i, j))

    def out_shape(dt, layout):
        return jax.ShapeDtypeStruct((m // tm, width, tm) if layout == "tiles_t" else (m, width), dt)

    in_specs = [pl.BlockSpec((tm, kd), lambda j, i: (i, 0))]
    operands = [a]
    if outs2:
        m2 = a2.shape[0]
        in_specs.append(pl.BlockSpec((m2, kd), lambda j, i: (0, 0)))
        operands.append(a2)
    in_specs.append(pl.BlockSpec((kd, tn), lambda j, i: (0, c0 + j)))
    operands.append(w)
    return pl.pallas_call(
        functools.partial(_proj_kernel, outs=outs, outs2=outs2, cast=cast),
        grid=(width // tn, m // tm),
        in_specs=in_specs,
        out_specs=([out_spec(layout) for _, _, layout in outs]
                   + [pl.BlockSpec((a2.shape[0], tn), lambda j, i: (0, j)) for _ in outs2]),
        out_shape=([out_shape(dt, layout) for dt, _, layout in outs]
                   + [jax.ShapeDtypeStruct((a2.shape[0], width), dt) for dt, _, _ in outs2]),
        scratch_shapes=[pltpu.VMEM((kd, tn), BF16)] if cast else [],
        compiler_params=_params("arbitrary", "arbitrary"),
        name=name,
    )(*operands)


def _kproj_kernel(a_ref, a2_ref, w_ref, k32_ref, k16_ref, k2_ref, wb_ref):
    first = pl.program_id(1) == 0

    @pl.when(first)
    def _():
        wb_ref[...] = w_ref[...].astype(BF16)
    wb = wb_ref[...]
    acc = jnp.dot(a_ref[...], wb, preferred_element_type=F32)
    k16_ref[...] = acc.astype(BF16)
    for hc in range(k32_ref.shape[1]):
        k32_ref[:, hc, :] = acc[:, hc * HEAD_DIM:(hc + 1) * HEAD_DIM]

    @pl.when(first)
    def _():
        k2_ref[...] = jnp.dot(a2_ref[...], wb, preferred_element_type=F32)


def _kproj(a, a2, w, col_start, width):
    m, kd = a.shape
    m2 = a2.shape[0]
    tm, tn = _tile(m, KPROJ_ROWS), _tile(width, KPROJ_COLS)
    assert col_start % tn == 0
    c0 = col_start // tn
    return pl.pallas_call(
        _kproj_kernel,
        grid=(width // tn, m // tm),
        in_specs=[pl.BlockSpec((tm, kd), lambda j, i: (i, 0)),
                  pl.BlockSpec((m2, kd), lambda j, i: (0, 0)),
                  pl.BlockSpec((kd, tn), lambda j, i: (0, c0 + j), pipeline_mode=pl.Buffered(1))],
        out_specs=[pl.BlockSpec((tm, tn // HEAD_DIM, HEAD_DIM), lambda j, i: (i, j, 0)),
                   pl.BlockSpec((tm, tn), lambda j, i: (i, j)),
                   pl.BlockSpec((m2, tn), lambda j, i: (0, j))],
        out_shape=[jax.ShapeDtypeStruct((m, width // HEAD_DIM, HEAD_DIM), F32),
                   jax.ShapeDtypeStruct((m, width), BF16),
                   jax.ShapeDtypeStruct((m2, width), F32)],
        scratch_shapes=[pltpu.VMEM((kd, tn), BF16)],
        compiler_params=_params("arbitrary", "arbitrary"),
        name="k_proj",
    )(a, a2, w)


def _hgrn_masks(c):
    row = lax.broadcasted_iota(jnp.int32, (c, HEAD_DIM), 0)
    ti = lax.broadcasted_iota(jnp.int32, (c, c), 0)
    si = lax.broadcasted_iota(jnp.int32, (c, c), 1)
    scan, levels = [], []
    sh = 1
    while sh < c:
        scan.append((sh, row >= sh))
        sh *= 2
    n, lg = 1, 0
    while n < c:
        u = ti >> lg
        w = si >> lg
        levels.append((n, ((u ^ w) * 2 + (u & 1)) == 3, (row & n) != 0))
        n *= 2
        lg += 1
    return scan, levels


def _hgrn_head(q, fr, v, z, lo, g, st, masks, *, c, layer):
    scan, levels = masks
    e = jnp.exp(lo - jnp.max(lo, axis=0, keepdims=True))
    lb = jnp.sum(e[:layer + 1], axis=0, keepdims=True) / jnp.sum(e, axis=0, keepdims=True)
    f = lb + (1.0 - lb) * jax.nn.sigmoid(fr)
    k = 1.0 - f

    b = jnp.log2(f)
    for sh, keep in scan:
        b = b + jnp.where(keep, pltpu.roll(b, sh, 0), 0.0)
    b_last = b[c - 1:c, :]

    vb = v.astype(BF16)
    o = lax.dot_general((q * jnp.exp2(b)).astype(BF16), st.astype(BF16), _NT,
                        preferred_element_type=F32)

    att = jnp.zeros((c, c), F32)
    bend = b
    for n, pair_mask, upper in levels:
        bstart = pltpu.roll(bend, n, 0)
        qn = (q * jnp.exp2(b - bstart)).astype(BF16)
        kn = (k * jnp.exp2(bend - b)).astype(BF16)
        an = lax.dot_general(qn, kn, _NT, preferred_element_type=F32)
        att = jnp.where(pair_mask, an, att)
        bend = jnp.where(upper, bend, pltpu.roll(bend, c - n, 0))
    diag = jnp.sum(q * k, axis=-1, keepdims=True)
    o = o + jnp.dot(att.astype(BF16), vb, preferred_element_type=F32) + diag * v

    ke = (k * jnp.exp2(b_last - b)).astype(BF16)
    st_new = jnp.exp2(b_last) * st + lax.dot_general(vb, ke, _TN, preferred_element_type=F32)
    on = o * lax.rsqrt(jnp.mean(o * o, axis=-1, keepdims=True) + EPS) * g
    return on * _silu(z), st_new


def _hgrn_kernel(q_ref, f_ref, i_ref, z_ref, lo_ref, g_ref, s0_ref, o_ref, s_ref, st_ref, *,
                 chunk, layer, hb):
    step = pl.program_id(2)
    last = step == pl.num_programs(2) - 1
    masks = _hgrn_masks(chunk)
    for hh in range(hb):
        sl = slice(hh * HEAD_DIM, (hh + 1) * HEAD_DIM)

        @pl.when(step == 0)
        def _():
            st_ref[hh] = s0_ref[0, hh].T

        out, st_new = _hgrn_head(q_ref[:, sl], f_ref[:, sl], i_ref[:, sl], z_ref[:, sl],
                                 lo_ref[:, sl], g_ref[...], st_ref[hh], masks, c=chunk, layer=layer)
        st_ref[hh] = st_new
        o_ref[:, sl] = out.astype(o_ref.dtype)

        @pl.when(last)
        def _():
            s_ref[0, hh] = st_new.T


def _hgrn(hg, *, hg_lower, hg_norm, s0, batch, seq, heads, layer, chunk, hb, out_dtype):
    assert out_dtype == F32 or chunk % 16 == 0
    nc = seq // chunk
    bw = hb * HEAD_DIM
    nhb = heads // hb

    def col(t):
        return pl.BlockSpec((chunk, bw), lambda b, h, n: (b * nc + n, t * nhb + h))

    state_spec = pl.BlockSpec((1, hb, HEAD_DIM, HEAD_DIM), lambda b, h, n: (b, h, 0, 0))
    return pl.pallas_call(
        functools.partial(_hgrn_kernel, chunk=chunk, layer=layer, hb=hb),
        grid=(batch, nhb, nc),
        in_specs=[col(0), col(1), col(2), col(3),
                  pl.BlockSpec((hg_lower.shape[0], bw), lambda b, h, n: (0, h)),
                  pl.BlockSpec((1, HEAD_DIM), lambda b, h, n: (0, 0)),
                  state_spec],
        out_specs=[pl.BlockSpec((chunk, bw), lambda b, h, n: (b * nc + n, h)), state_spec],
        out_shape=[jax.ShapeDtypeStruct((batch * seq, heads * HEAD_DIM), out_dtype),
                   jax.ShapeDtypeStruct((batch, heads, HEAD_DIM, HEAD_DIM), F32)],
        scratch_shapes=[pltpu.VMEM((hb, HEAD_DIM, HEAD_DIM), F32)],
        compiler_params=_params("arbitrary", "arbitrary", "arbitrary"),
        name="hgrn2",
    )(hg, hg, hg, hg, hg_lower, hg_norm.reshape(1, HEAD_DIM), s0)


def _rel_bias_tile(rel, rb_ref, h):
    max_exact = REL_BUCKETS // 2
    n = jnp.maximum(rel, 0)
    nf = jnp.maximum(n, 1).astype(F32)
    large = max_exact + (jnp.log(nf / max_exact) / math.log(REL_MAX_DIST / max_exact)
                         * (REL_BUCKETS - max_exact)).astype(jnp.int32)
    bucket = jnp.where(n < max_exact, n, jnp.minimum(large, REL_BUCKETS - 1))
    far = rb_ref[REL_BUCKETS - 1, h]
    tile = jnp.zeros(rel.shape, F32)
    for bk in range(REL_BUCKETS - 1):
        tile = jnp.where(bucket == bk, (rb_ref[bk, h] - far) * LOG2E, tile)
    return tile


def _lambda(lam_ref, lam_init):
    lv = lam_ref[...]
    return (jnp.exp(jnp.sum(lv[0:1] * lv[1:2], axis=-1, keepdims=True))
            - jnp.exp(jnp.sum(lv[2:3] * lv[3:4], axis=-1, keepdims=True)) + lam_init)


def _diff_out(o1, o2, lam, g, z, lam_init):
    o = o1 - lam * o2
    on = o * lax.rsqrt(jnp.mean(o * o, axis=-1, keepdims=True) + EPS) * g * (1.0 - lam_init)
    return on * _silu(z)


def _attn_prompt_kernel(rb_ref, q_ref, k_ref, vt_ref, z_ref, lam_ref, g_ref, o_ref,
                        m_ref, l_ref, acc_ref, bias_ref, *, tile, lam_init):
    t = tile
    h, i = pl.program_id(0), pl.program_id(1)

    @pl.when(i == 0)
    def _():
        blk = REL_MAX_DIST
        nb = t // blk
        si = lax.broadcasted_iota(jnp.int32, (blk, blk), 0)
        ti = lax.broadcasted_iota(jnp.int32, (blk, blk), 1)
        on_diag = jnp.where(si <= ti, _rel_bias_tile(ti - si, rb_ref, h), NEG)
        next_blk = _rel_bias_tile(ti - si + blk, rb_ref, h)
        zero = jnp.zeros((blk, blk), F32)
        masked = jnp.full((blk, blk), NEG, F32)
        for kb in range(nb):
            for qb in range(nb):
                rows, cols = slice(kb * blk, (kb + 1) * blk), slice(qb * blk, (qb + 1) * blk)
                bias_ref[0, rows, cols] = (on_diag if qb == kb else next_blk if qb == kb + 1
                                           else zero if qb > kb else masked)
                bias_ref[1, rows, cols] = next_blk if (kb == nb - 1 and qb == 0) else zero

    m_ref[...] = jnp.full(m_ref.shape, NEG, F32)
    l_ref[...] = jnp.zeros(l_ref.shape, F32)
    acc_ref[...] = jnp.zeros(acc_ref.shape, F32)

    def update(j, bias, krows=None, lanes=None):
        krows = slice(0, t) if krows is None else krows
        lanes = slice(0, t) if lanes is None else lanes
        k = k_ref[j][krows, :]
        vt = vt_ref[j][:, krows]
        sts = []
        for c in range(2):
            sl = slice(c * HEAD_DIM, (c + 1) * HEAD_DIM)
            st = lax.dot_general(k[:, sl], q_ref[lanes, sl], _NT, preferred_element_type=F32)
            sts.append(st if bias is None else st + bias)
        ps = []
        for c in range(2):
            m_prev = m_ref[c, :, lanes]
            m_new = jnp.maximum(m_prev, jnp.max(sts[c], axis=0, keepdims=True))
            alpha = jnp.exp2(m_prev - m_new)
            p = jnp.exp2(sts[c] - m_new)
            l_ref[c, :, lanes] = alpha * l_ref[c, :, lanes] + jnp.sum(p, axis=0, keepdims=True)
            m_ref[c, :, lanes] = m_new
            ps.append((alpha, p.astype(BF16)))
        for c in range(2):
            alpha, pb = ps[c]
            acc_ref[c, :, lanes] = alpha * acc_ref[c, :, lanes] + jnp.dot(vt, pb, preferred_element_type=F32)

    def update2(j0):
        kk = [k_ref[j0], k_ref[j0 + 1]]
        vts = [vt_ref[j0], vt_ref[j0 + 1]]
        sts = [[lax.dot_general(kk[u][:, c * HEAD_DIM:(c + 1) * HEAD_DIM],
                                q_ref[:, c * HEAD_DIM:(c + 1) * HEAD_DIM], _NT,
                                preferred_element_type=F32) for u in range(2)] for c in range(2)]
        ps = []
        for c in range(2):
            m_prev = m_ref[c]
            m_new = jnp.maximum(m_prev, jnp.maximum(jnp.max(sts[c][0], axis=0, keepdims=True),
                                                    jnp.max(sts[c][1], axis=0, keepdims=True)))
            alpha = jnp.exp2(m_prev - m_new)
            p0 = jnp.exp2(sts[c][0] - m_new)
            p1 = jnp.exp2(sts[c][1] - m_new)
            l_ref[c] = (alpha * l_ref[c] + jnp.sum(p0, axis=0, keepdims=True)
                        + jnp.sum(p1, axis=0, keepdims=True))
            m_ref[c] = m_new
            ps.append((alpha, p0.astype(BF16), p1.astype(BF16)))
        for c in range(2):
            alpha, p0, p1 = ps[c]
            acc_ref[c] = (alpha * acc_ref[c] + jnp.dot(vts[0], p0, preferred_element_type=F32)
                          + jnp.dot(vts[1], p1, preferred_element_type=F32))

    n_far = jnp.maximum(i - 1, 0)

    def far_pair(jj, carry):
        update2(2 * jj)
        return carry

    lax.fori_loop(0, n_far // 2, far_pair, 0)

    @pl.when(n_far % 2 == 1)
    def _():
        update(n_far - 1, None)

    @pl.when(i >= 1)
    def _():
        update(i - 1, bias_ref[1])

    half = t // 2
    update(i, bias_ref[0, :half, :], krows=slice(0, half))
    update(i, bias_ref[0, half:, half:], krows=slice(half, t), lanes=slice(half, t))
    lam = _lambda(lam_ref, lam_init)
    ot = acc_ref[0] / l_ref[0] - lam * (acc_ref[1] / l_ref[1])
    o = ot.T
    on = o * lax.rsqrt(jnp.mean(o * o, axis=-1, keepdims=True) + EPS) * g_ref[...] * (1.0 - lam_init)
    o_ref[...] = (on * _silu(z_ref[...])).astype(o_ref.dtype)


def _attn_prompt(aq, ak, avt, z, z_col, rel_bias, da_lambda, da_subln, *, seq, heads, lam_init):
    hw = 2 * HEAD_DIM
    nq, _, tile = avt.shape
    assert tile % REL_MAX_DIST == 0 and nq * tile == seq
    cz = z_col // hw
    return pl.pallas_call(
        functools.partial(_attn_prompt_kernel, tile=tile, lam_init=lam_init),
        grid=(heads, nq),
        in_specs=[pl.BlockSpec(memory_space=pltpu.SMEM),
                  pl.BlockSpec((tile, hw), lambda h, i: (i, h)),
                  pl.BlockSpec((nq, tile, hw), lambda h, i: (0, 0, h)),
                  pl.BlockSpec((nq, hw, tile), lambda h, i: (0, h, 0)),
                  pl.BlockSpec((tile, hw), lambda h, i: (i, cz + h)),
                  pl.BlockSpec((4, HEAD_DIM), lambda h, i: (0, 0)),
                  pl.BlockSpec((1, hw), lambda h, i: (0, 0))],
        out_specs=pl.BlockSpec((tile, hw), lambda h, i: (i, h)),
        out_shape=jax.ShapeDtypeStruct((seq, heads * hw), BF16),
        scratch_shapes=[pltpu.VMEM((2, 1, tile), F32), pltpu.VMEM((2, 1, tile), F32),
                        pltpu.VMEM((2, hw, tile), F32), pltpu.VMEM((2, tile, tile), F32)],
        compiler_params=_params("arbitrary", "arbitrary"),
        name="attn_prompt",
    )(rel_bias, aq, ak.reshape(nq, tile, heads * hw), avt, z, da_lambda, da_subln.reshape(1, hw))


def _attn_decode_kernel(pt_ref, rb_ref, q_ref, kn_ref, vn_ref, z_ref, *rest,
                        heads, page, n_pages, dec, group, lam_init):
    del pt_ref
    kc_refs, vc_refs = rest[:group], rest[group:2 * group]
    lam_ref, g_ref, o_ref, qx_ref, m_ref, l_ref, acc_ref, bias_ref, nbias_ref = rest[2 * group:]
    b, p = pl.program_id(0), pl.program_id(1)
    hw = 2 * HEAD_DIM
    r = heads * dec
    n = heads * page
    n_groups = n_pages // group
    past = n_pages * page

    def near_bias(rel):
        return jnp.concatenate([_rel_bias_tile(rel[h * dec:(h + 1) * dec], rb_ref, h)
                                for h in range(heads)], axis=0)

    @pl.when((b == 0) & (p == 0))
    def _():
        row = lax.broadcasted_iota(jnp.int32, (r, n), 0)
        col = lax.broadcasted_iota(jnp.int32, (r, n), 1)
        valid = (col % heads) == (row // dec)
        bias_ref[0] = jnp.where(valid, 0.0, NEG)
        rel = (past + row % dec) - ((n_pages - 1) * page + col // heads)
        bias_ref[1] = jnp.where(valid, near_bias(rel), NEG)
        row = lax.broadcasted_iota(jnp.int32, (r, r), 0)
        col = lax.broadcasted_iota(jnp.int32, (r, r), 1)
        rel = row % dec - col % dec
        valid = jnp.where((col // dec) == (row // dec), rel, -1) >= 0
        nbias_ref[...] = jnp.where(valid, near_bias(rel), NEG)

    @pl.when(p == 0)
    def _():
        m_ref[...] = jnp.full(m_ref.shape, NEG, F32)
        l_ref[...] = jnp.zeros(l_ref.shape, F32)
        acc_ref[...] = jnp.zeros(acc_ref.shape, F32)
        for c in range(2):
            qx_ref[c] = jnp.concatenate(
                [q_ref[:, (2 * h + c) * HEAD_DIM:(2 * h + c + 1) * HEAD_DIM] for h in range(heads)],
                axis=0).astype(BF16)

    def attend(tiles):
        scores = [[lax.dot_general(qx_ref[c], keys_of(c).astype(BF16), _NT,
                                   preferred_element_type=F32) + bias for keys_of, _, bias in tiles]
                  for c in range(2)]
        ps, alphas = [], []
        for c in range(2):
            rows = slice(c * r, (c + 1) * r)
            m_prev = m_ref[rows]
            m_new = m_prev
            for s in scores[c]:
                m_new = jnp.maximum(m_new, jnp.max(s, axis=-1, keepdims=True))
            alpha = jnp.exp2(m_prev - m_new)
            prs = [jnp.exp2(s - m_new) for s in scores[c]]
            rowsum = jnp.sum(prs[0], axis=-1, keepdims=True)
            for pr in prs[1:]:
                rowsum = rowsum + jnp.sum(pr, axis=-1, keepdims=True)
            l_ref[rows] = alpha * l_ref[rows] + rowsum
            m_ref[rows] = m_new
            ps.append([pr.astype(BF16) for pr in prs])
            alphas.append(alpha)
        pv = None
        for ti, (_, vb, _) in enumerate(tiles):
            part = jnp.dot(jnp.concatenate([ps[0][ti], ps[1][ti]], axis=0), vb,
                           preferred_element_type=F32)
            pv = part if pv is None else pv + part
        acc_ref[...] = jnp.concatenate(alphas, axis=0) * acc_ref[...] + pv

    def page_tiles(last_bias):
        tiles = []
        for gi in range(group):
            kc_ref = kc_refs[gi]
            bias = bias_ref[last_bias if gi == group - 1 else 0]
            tiles.append((lambda c, kc_ref=kc_ref: kc_ref[pl.ds(c, n, stride=2), :],
                          vc_refs[gi][...].astype(BF16), bias))
        return tiles

    @pl.when(p < n_groups - 1)
    def _():
        attend(page_tiles(0))

    @pl.when(p == n_groups - 1)
    def _():
        attend(page_tiles(1))

    @pl.when(p == n_groups)
    def _():
        def new_keys(c):
            return jnp.concatenate(
                [kn_ref[:, (2 * h + c) * HEAD_DIM:(2 * h + c + 1) * HEAD_DIM] for h in range(heads)],
                axis=0)
        vnew = jnp.concatenate([vn_ref[:, h * hw:(h + 1) * hw] for h in range(heads)], axis=0)
        attend([(new_keys, vnew.astype(BF16), nbias_ref[...])])
        lam = _lambda(lam_ref, lam_init)
        acc = acc_ref[...] / l_ref[...]
        for h in range(heads):
            sl = slice(h * hw, (h + 1) * hw)
            o_ref[:, sl] = _diff_out(acc[h * dec:(h + 1) * dec], acc[r + h * dec:r + (h + 1) * dec],
                                     lam, g_ref[...], z_ref[:, sl], lam_init).astype(o_ref.dtype)


def _attn_decode(aq, ak, av, z, z_col, cache_k, cache_v, page_table, rel_bias, da_lambda, da_subln, *,
                 batch, dec, heads, lam_init):
    hw = 2 * HEAD_DIM
    w = heads * hw
    n_pool, page = cache_k.shape[0], cache_k.shape[1]
    n_pages = page_table.shape[1]
    group = math.gcd(n_pages, PAGES_PER_STEP)
    n_groups = n_pages // group
    assert page >= REL_MAX_DIST and dec <= REL_MAX_DIST
    kc = cache_k.reshape(n_pool, page * heads * 2, HEAD_DIM)
    vc = cache_v.reshape(n_pool, page * heads, hw)
    r = heads * dec

    def cache_spec(rows, width, gi):
        return pl.BlockSpec(
            (None, rows, width),
            lambda b, p, pt: (pt[b, jnp.minimum(p, n_groups - 1) * group + gi], 0, 0))

    def row_spec(cb):
        return pl.BlockSpec((dec, w), lambda b, p, pt: (b, cb))

    grid_spec = pltpu.PrefetchScalarGridSpec(
        num_scalar_prefetch=1,
        grid=(batch, n_groups + 1),
        in_specs=([pl.BlockSpec(memory_space=pltpu.SMEM),
                   row_spec(0), row_spec(0), row_spec(0), row_spec(z_col // w)]
                  + [cache_spec(page * heads * 2, HEAD_DIM, gi) for gi in range(group)]
                  + [cache_spec(page * heads, hw, gi) for gi in range(group)]
                  + [pl.BlockSpec((4, HEAD_DIM), lambda b, p, pt: (0, 0)),
                     pl.BlockSpec((1, hw), lambda b, p, pt: (0, 0))]),
        out_specs=pl.BlockSpec((dec, w), lambda b, p, pt: (b, 0)),
        scratch_shapes=[pltpu.VMEM((2, r, HEAD_DIM), BF16),
                        pltpu.VMEM((2 * r, 1), F32), pltpu.VMEM((2 * r, 1), F32),
                        pltpu.VMEM((2 * r, hw), F32),
                        pltpu.VMEM((2, r, heads * page), F32), pltpu.VMEM((r, r), F32)],
    )
    return pl.pallas_call(
        functools.partial(_attn_decode_kernel, heads=heads, page=page, n_pages=n_pages, dec=dec,
                          group=group, lam_init=lam_init),
        grid_spec=grid_spec,
        out_shape=jax.ShapeDtypeStruct((batch * dec, w), F32),
        compiler_params=_params("arbitrary", "arbitrary"),
        name="attn_decode",
    )(page_table, rel_bias, aq, ak, av, z, *([kc] * group), *([vc] * group),
      da_lambda, da_subln.reshape(1, hw))


def _up_kernel(oh_ref, oa_ref, wh_ref, wa_ref, gh_ref, ga_ref, o_ref, whb_ref, wab_ref):
    @pl.when(pl.program_id(1) == 0)
    def _():
        whb_ref[...] = wh_ref[...].astype(BF16)
        wab_ref[...] = wa_ref[...].astype(BF16)

    uh = jnp.dot(oh_ref[...].astype(BF16), whb_ref[...], preferred_element_type=F32)
    ua = jnp.dot(oa_ref[...].astype(BF16), wab_ref[...], preferred_element_type=F32)
    o_ref[...] = (jax.nn.sigmoid(gh_ref[...]) * uh + jax.nn.sigmoid(ga_ref[...]) * ua).astype(o_ref.dtype)


def _up_merge(o_h, o_a, w_h, w_a, gates, gh_col, ga_col):
    m, kh = o_h.shape
    ka = o_a.shape[1]
    d = w_h.shape[1]
    tm, tn = _tile(m, ROW_TILE), _tile(d, PROJ_COLS)
    cgh, cga = gh_col // tn, ga_col // tn
    return pl.pallas_call(
        _up_kernel,
        grid=(d // tn, m // tm),
        in_specs=[pl.BlockSpec((tm, kh), lambda j, i: (i, 0)),
                  pl.BlockSpec((tm, ka), lambda j, i: (i, 0)),
                  pl.BlockSpec((kh, tn), lambda j, i: (0, j)),
                  pl.BlockSpec((ka, tn), lambda j, i: (0, j)),
                  pl.BlockSpec((tm, tn), lambda j, i: (i, cgh + j)),
                  pl.BlockSpec((tm, tn), lambda j, i: (i, cga + j))],
        out_specs=pl.BlockSpec((tm, tn), lambda j, i: (i, j)),
        out_shape=jax.ShapeDtypeStruct((m, d), BF16),
        scratch_shapes=[pltpu.VMEM((kh, tn), BF16), pltpu.VMEM((ka, tn), BF16)],
        compiler_params=_params("arbitrary", "arbitrary"),
        name="up_merge",
    )(o_h, o_a, w_h, w_a, gates, gates)


def _out_kernel(a_ref, w_ref, x_ref, g_ref, y_ref):
    out = jnp.dot(a_ref[...], w_ref[...], preferred_element_type=F32)
    ms = jnp.mean(out * out, axis=-1, keepdims=True)
    y_ref[...] = x_ref[...] + out * lax.rsqrt(ms + EPS) * g_ref[...]


def _out_proj(merged, w_out, x, g_post):
    m, d = x.shape
    tm = _tile(m, OUT_ROWS)
    return pl.pallas_call(
        _out_kernel,
        grid=(m // tm,),
        in_specs=[pl.BlockSpec((tm, d), lambda i: (i, 0)),
                  pl.BlockSpec((d, d), lambda i: (0, 0), pipeline_mode=pl.Buffered(1)),
                  pl.BlockSpec((tm, d), lambda i: (i, 0)),
                  pl.BlockSpec((1, d), lambda i: (0, 0))],
        out_specs=pl.BlockSpec((tm, d), lambda i: (i, 0)),
        out_shape=jax.ShapeDtypeStruct((m, d), F32),
        compiler_params=_params("arbitrary"),
        name="out_proj",
    )(merged, w_out, x, g_post.reshape(1, d))


def _mix_and_project(x2, o_h, o_a, gates, w, da_w):
    d = x2.shape[1]
    merged = _up_merge(o_h, o_a, w["w_up_hg"], w["w_up_da"], gates, da_w, da_w + d)
    return _out_proj(merged, w["w_out"], x2, w["g_post"])


def _layer(xp, xs, s0s, cache_k, cache_v, page_table, w, *, layer, lam_init):
    _, seq, d = xp.shape
    bsz, dec, _ = xs.shape
    assert xp.shape[0] == 1
    hg_heads = (d // 2) // HEAD_DIM
    da_heads = (d // 2) // (2 * HEAD_DIM)
    hg_w = hg_heads * HEAD_DIM
    da_w = da_heads * 2 * HEAD_DIM
    col_aq = 4 * hg_w
    qscale = LOG2E * HEAD_DIM ** -0.5
    f32 = (F32, 1.0, "rows")

    xp2, xs2 = xp.reshape(seq, d), xs.reshape(bsz * dec, d)
    xnp, xns = _rmsnorm_cast(xp2, w["g_pre"]), _rmsnorm_cast(xs2, w["g_pre"])
    proj = functools.partial(_proj, xnp, w["w_in"], tn_pref=PROJ_COLS, name="in_proj", a2=xns)
    hg_p, hg_s = proj(0, 4 * hg_w, [f32], outs2=[f32])
    aq_p, aq_s = proj(col_aq, da_w, [(BF16, qscale, "rows")], outs2=[(F32, qscale, "rows")])
    k_p, k16_p, k_s = _kproj(xnp, xns, w["w_in"], col_aq + da_w, da_w)
    v_p, vt16_p, v_s = proj(col_aq + 2 * da_w, da_w, [f32, (BF16, 1.0, "tiles_t")], outs2=[f32])
    gates_p, gates_s = proj(col_aq + 3 * da_w, da_w + 2 * d, [f32], outs2=[f32])

    s0p = jnp.zeros((1, hg_heads, HEAD_DIM, HEAD_DIM), F32)
    hgrn = functools.partial(_hgrn, hg_lower=w["hg_lower"], hg_norm=w["hg_norm"], heads=hg_heads, layer=layer,
                             hb=math.gcd(hg_heads, HGRN_HEADS))
    oh_p, sp = hgrn(hg_p, s0=s0p, batch=1, seq=seq, chunk=math.gcd(seq, HGRN_CHUNK), out_dtype=BF16)
    oh_s, ss = hgrn(hg_s, s0=s0s, batch=bsz, seq=dec, chunk=math.gcd(dec, HGRN_DEC_CHUNK), out_dtype=F32)

    attn_w = (w["rel_bias"], w["da_lambda"], w["da_subln"])
    oa_p = _attn_prompt(aq_p, k16_p, vt16_p, gates_p, 0, *attn_w, seq=seq, heads=da_heads, lam_init=lam_init)
    oa_s = _attn_decode(aq_s, k_s, v_s, gates_s, 0, cache_k, cache_v, page_table, *attn_w, batch=bsz,
                        dec=dec, heads=da_heads, lam_init=lam_init)

    yp = _mix_and_project(xp2, oh_p, oa_p, gates_p, w, da_w).reshape(xp.shape)
    ys = _mix_and_project(xs2, oh_s, oa_s, gates_s, w, da_w).reshape(xs.shape)
    kv_p = (k_p.reshape(1, seq, da_heads, 2, HEAD_DIM), v_p.reshape(1, seq, da_heads, 2 * HEAD_DIM))
    kv_s = (k_s.reshape(bsz, dec, da_heads, 2, HEAD_DIM), v_s.reshape(bsz, dec, da_heads, 2 * HEAD_DIM))
    return yp, ys, kv_p + (sp,), kv_s + (ss,)


def kernel(x_prompt, x_sample, cache_k, cache_v, state_hgrn, page_table, norm_pre, norm_post, w_in,
           hg_lower, hg_norm, da_lambda, da_subln, rel_bias, w_up_hg, w_up_da, w_out):
    depth = w_in.shape[0]
    hp, hs = x_prompt, x_sample
    outs = [[] for _ in range(6)]
    for l in range(depth):
        lam_init = 0.8 - 0.6 * math.exp(-0.3 * l)
        w = dict(g_pre=norm_pre[l], g_post=norm_post[l], w_in=w_in[l], hg_lower=hg_lower,
                 hg_norm=hg_norm[l], da_lambda=da_lambda[l], da_subln=da_subln[l], rel_bias=rel_bias,
                 w_up_hg=w_up_hg[l], w_up_da=w_up_da[l], w_out=w_out[l].astype(BF16))
        hp, hs, new_p, new_s = _layer(hp, hs, state_hgrn[l], cache_k[l], cache_v[l], page_table, w,
                                      layer=l, lam_init=lam_init)
        for lst, val in zip(outs, new_p + new_s):
            lst.append(val)
    return (hp, hs) + tuple(jnp.stack(o) for o in outs)
```

```python
import functools
import math

import jax
import jax.numpy as jnp
from jax import lax
from jax.experimental import pallas as pl
from jax.experimental.pallas import tpu as pltpu

HEAD_DIM = 128
REL_BUCKETS = 32
REL_MAX_DIST = 128
EPS = 1e-6
NEG = -1e30
LOG2E = math.log2(math.e)
VMEM_LIMIT = 56 * 1024 * 1024
BF16 = jnp.bfloat16
F32 = jnp.float32

ROW_TILE = 1024
PROJ_COLS = 512
KPROJ_ROWS = 512
KPROJ_COLS = 1024
NORM_ROWS = 256
OUT_ROWS = 128
HGRN_CHUNK = 256
HGRN_DEC_CHUNK = 64
HGRN_HEADS = 16
PAGES_PER_STEP = 4
PAGE_RING = 4

_NT = (((1,), (1,)), ((), ()))
_TN = (((0,), (0,)), ((), ()))


def _params(*sem):
    return pltpu.CompilerParams(dimension_semantics=sem, vmem_limit_bytes=VMEM_LIMIT)


def _silu(z):
    return z * jax.nn.sigmoid(z)


def _tile(n, pref):
    t = min(n, pref)
    assert n % t == 0, (n, t)
    return t


def _rmsnorm_kernel(x_ref, g_ref, o_ref):
    x = x_ref[...]
    ms = jnp.mean(x * x, axis=-1, keepdims=True)
    o_ref[...] = (x * lax.rsqrt(ms + EPS) * g_ref[...]).astype(o_ref.dtype)


def _rmsnorm_cast(x, g):
    m, d = x.shape
    tm = _tile(m, NORM_ROWS)
    return pl.pallas_call(
        _rmsnorm_kernel,
        grid=(m // tm,),
        in_specs=[pl.BlockSpec((tm, d), lambda i: (i, 0)),
                  pl.BlockSpec((1, d), lambda i: (0, 0))],
        out_specs=pl.BlockSpec((tm, d), lambda i: (i, 0)),
        out_shape=jax.ShapeDtypeStruct((m, d), BF16),
        compiler_params=_params("arbitrary"),
        name="rmsnorm_cast",
    )(x, g.reshape(1, d))


def _store_products(acc, out_refs, outs):
    for o_ref, (_, scale, layout) in zip(out_refs, outs):
        val = acc if scale == 1.0 else acc * scale
        o_ref[...] = (val.T if layout == "tiles_t" else val).astype(o_ref.dtype)


def _proj_kernel(*refs, outs, outs2, cast):
    refs = list(refs)
    a_ref = refs.pop(0)
    a2_ref = refs.pop(0) if outs2 else None
    w_ref = refs.pop(0)
    wb_ref = refs.pop() if cast else None
    out_refs, out2_refs = refs[:len(outs)], refs[len(outs):]
    first = pl.program_id(1) == 0

    if cast:
        @pl.when(first)
        def _():
            wb_ref[...] = w_ref[...].astype(BF16)
        wb = wb_ref[...]
    else:
        wb = w_ref[...]

    _store_products(jnp.dot(a_ref[...], wb, preferred_element_type=F32), out_refs, outs)
    if outs2:
        @pl.when(first)
        def _():
            _store_products(jnp.dot(a2_ref[...], wb, preferred_element_type=F32), out2_refs, outs2)


def _proj(a, w, col_start, width, outs, *, tn_pref, name, a2=None, outs2=()):
    m, kd = a.shape
    tm, tn = _tile(m, ROW_TILE), _tile(width, tn_pref)
    assert col_start % tn == 0
    c0 = col_start // tn
    cast = w.dtype != BF16
    outs, outs2 = tuple(outs), tuple(outs2)
    assert all(layout == "rows" for _, _, layout in outs2)

    def out_spec(layout):
        if layout == "tiles_t":
            return pl.BlockSpec((None, tn, tm), lambda j, i: (i, j, 0))
        return pl.BlockSpec((tm, tn), lambda j, i: (i, j))

    def out_shape(dt, layout):
        return jax.ShapeDtypeStruct((m // tm, width, tm) if layout == "tiles_t" else (m, width), dt)

    in_specs = [pl.BlockSpec((tm, kd), lambda j, i: (i, 0))]
    operands = [a]
    if outs2:
        m2 = a2.shape[0]
        in_specs.append(pl.BlockSpec((m2, kd), lambda j, i: (0, 0)))
        operands.append(a2)
    in_specs.append(pl.BlockSpec((kd, tn), lambda j, i: (0, c0 + j)))
    operands.append(w)
    return pl.pallas_call(
        functools.partial(_proj_kernel, outs=outs, outs2=outs2, cast=cast),
        grid=(width // tn, m // tm),
        in_specs=in_specs,
        out_specs=([out_spec(layout) for _, _, layout in outs]
                   + [pl.BlockSpec((a2.shape[0], tn), lambda j, i: (0, j)) for _ in outs2]),
        out_shape=([out_shape(dt, layout) for dt, _, layout in outs]
                   + [jax.ShapeDtypeStruct((a2.shape[0], width), dt) for dt, _, _ in outs2]),
        scratch_shapes=[pltpu.VMEM((kd, tn), BF16)] if cast else [],
        compiler_params=_params("arbitrary", "arbitrary"),
        name=name,
    )(*operands)


def _kproj_kernel(a_ref, a2_ref, w_ref, k32_ref, k16_ref, k2_ref, wb_ref):
    first = pl.program_id(1) == 0

    @pl.when(first)
    def _():
        wb_ref[...] = w_ref[...].astype(BF16)
    wb = wb_ref[...]
    acc = jnp.dot(a_ref[...], wb, preferred_element_type=F32)
    k16_ref[...] = acc.astype(BF16)
    for hc in range(k32_ref.shape[1]):
        k32_ref[:, hc, :] = acc[:, hc * HEAD_DIM:(hc + 1) * HEAD_DIM]

    @pl.when(first)
    def _():
        k2_ref[...] = jnp.dot(a2_ref[...], wb, preferred_element_type=F32)


def _kproj(a, a2, w, col_start, width):
    m, kd = a.shape
    m2 = a2.shape[0]
    tm, tn = _tile(m, KPROJ_ROWS), _tile(width, KPROJ_COLS)
    assert col_start % tn == 0
    c0 = col_start // tn
    return pl.pallas_call(
        _kproj_kernel,
        grid=(width // tn, m // tm),
        in_specs=[pl.BlockSpec((tm, kd), lambda j, i: (i, 0)),
                  pl.BlockSpec((m2, kd), lambda j, i: (0, 0)),
                  pl.BlockSpec((kd, tn), lambda j, i: (0, c0 + j), pipeline_mode=pl.Buffered(1))],
        out_specs=[pl.BlockSpec((tm, tn // HEAD_DIM, HEAD_DIM), lambda j, i: (i, j, 0)),
                   pl.BlockSpec((tm, tn), lambda j, i: (i, j)),
                   pl.BlockSpec((m2, tn), lambda j, i: (0, j))],
        out_shape=[jax.ShapeDtypeStruct((m, width // HEAD_DIM, HEAD_DIM), F32),
                   jax.ShapeDtypeStruct((m, width), BF16),
                   jax.ShapeDtypeStruct((m2, width), F32)],
        scratch_shapes=[pltpu.VMEM((kd, tn), BF16)],
        compiler_params=_params("arbitrary", "arbitrary"),
        name="k_proj",
    )(a, a2, w)


def _hgrn_masks(c):
    row = lax.broadcasted_iota(jnp.int32, (c, HEAD_DIM), 0)
    ti = lax.broadcasted_iota(jnp.int32, (c, c), 0)
    si = lax.broadcasted_iota(jnp.int32, (c, c), 1)
    scan, levels = [], []
    sh = 1
    while sh < c:
        scan.append((sh, row >= sh))
        sh *= 2
    n, lg = 1, 0
    while n < c:
        u = ti >> lg
        w = si >> lg
        levels.append((n, ((u ^ w) * 2 + (u & 1)) == 3, (row & n) != 0))
        n *= 2
        lg += 1
    return scan, levels


def _hgrn_head(q, fr, v, z, lo, g, st, masks, *, c, layer):
    scan, levels = masks
    e = jnp.exp(lo - jnp.max(lo, axis=0, keepdims=True))
    lb = jnp.sum(e[:layer + 1], axis=0, keepdims=True) / jnp.sum(e, axis=0, keepdims=True)
    f = lb + (1.0 - lb) * jax.nn.sigmoid(fr)
    k = 1.0 - f

    b = jnp.log2(f)
    for sh, keep in scan:
        b = b + jnp.where(keep, pltpu.roll(b, sh, 0), 0.0)
    b_last = b[c - 1:c, :]

    vb = v.astype(BF16)
    o = lax.dot_general((q * jnp.exp2(b)).astype(BF16), st.astype(BF16), _NT,
                        preferred_element_type=F32)

    att = jnp.zeros((c, c), F32)
    bend = b
    for n, pair_mask, upper in levels:
        bstart = pltpu.roll(bend, n, 0)
        qn = (q * jnp.exp2(b - bstart)).astype(BF16)
        kn = (k * jnp.exp2(bend - b)).astype(BF16)
        an = lax.dot_general(qn, kn, _NT, preferred_element_type=F32)
        att = jnp.where(pair_mask, an, att)
        bend = jnp.where(upper, bend, pltpu.roll(bend, c - n, 0))
    diag = jnp.sum(q * k, axis=-1, keepdims=True)
    o = o + jnp.dot(att.astype(BF16), vb, preferred_element_type=F32) + diag * v

    ke = (k * jnp.exp2(b_last - b)).astype(BF16)
    st_new = jnp.exp2(b_last) * st + lax.dot_general(vb, ke, _TN, preferred_element_type=F32)
    on = o * lax.rsqrt(jnp.mean(o * o, axis=-1, keepdims=True) + EPS) * g
    return on * _silu(z), st_new


def _hgrn_kernel(q_ref, f_ref, i_ref, z_ref, lo_ref, g_ref, s0_ref, o_ref, s_ref, st_ref, *,
                 chunk, layer, hb):
    step = pl.program_id(2)
    last = step == pl.num_programs(2) - 1
    masks = _hgrn_masks(chunk)
    for hh in range(hb):
        sl = slice(hh * HEAD_DIM, (hh + 1) * HEAD_DIM)

        @pl.when(step == 0)
        def _():
            st_ref[hh] = s0_ref[0, hh].T

        out, st_new = _hgrn_head(q_ref[:, sl], f_ref[:, sl], i_ref[:, sl], z_ref[:, sl],
                                 lo_ref[:, sl], g_ref[...], st_ref[hh], masks, c=chunk, layer=layer)
        st_ref[hh] = st_new
        o_ref[:, sl] = out.astype(o_ref.dtype)

        @pl.when(last)
        def _():
            s_ref[0, hh] = st_new.T


def _hgrn(hg, *, hg_lower, hg_norm, s0, batch, seq, heads, layer, chunk, hb, out_dtype):
    assert out_dtype == F32 or chunk % 16 == 0
    nc = seq // chunk
    bw = hb * HEAD_DIM
    nhb = heads // hb

    def col(t):
        return pl.BlockSpec((chunk, bw), lambda b, h, n: (b * nc + n, t * nhb + h))

    state_spec = pl.BlockSpec((1, hb, HEAD_DIM, HEAD_DIM), lambda b, h, n: (b, h, 0, 0))
    return pl.pallas_call(
        functools.partial(_hgrn_kernel, chunk=chunk, layer=layer, hb=hb),
        grid=(batch, nhb, nc),
        in_specs=[col(0), col(1), col(2), col(3),
                  pl.BlockSpec((hg_lower.shape[0], bw), lambda b, h, n: (0, h)),
                  pl.BlockSpec((1, HEAD_DIM), lambda b, h, n: (0, 0)),
                  state_spec],
        out_specs=[pl.BlockSpec((chunk, bw), lambda b, h, n: (b * nc + n, h)), state_spec],
        out_shape=[jax.ShapeDtypeStruct((batch * seq, heads * HEAD_DIM), out_dtype),
                   jax.ShapeDtypeStruct((batch, heads, HEAD_DIM, HEAD_DIM), F32)],
        scratch_shapes=[pltpu.VMEM((hb, HEAD_DIM, HEAD_DIM), F32)],
        compiler_params=_params("arbitrary", "arbitrary", "arbitrary"),
        name="hgrn2",
    )(hg, hg, hg, hg, hg_lower, hg_norm.reshape(1, HEAD_DIM), s0)


def _rel_bias_tile(rel, rb_ref, h):
    max_exact = REL_BUCKETS // 2
    n = jnp.maximum(rel, 0)
    nf = jnp.maximum(n, 1).astype(F32)
    large = max_exact + (jnp.log(nf / max_exact) / math.log(REL_MAX_DIST / max_exact)
                         * (REL_BUCKETS - max_exact)).astype(jnp.int32)
    bucket = jnp.where(n < max_exact, n, jnp.minimum(large, REL_BUCKETS - 1))
    far = rb_ref[REL_BUCKETS - 1, h]
    tile = jnp.zeros(rel.shape, F32)
    for bk in range(REL_BUCKETS - 1):
        tile = jnp.where(bucket == bk, (rb_ref[bk, h] - far) * LOG2E, tile)
    return tile


def _lambda(lam_ref, lam_init):
    lv = lam_ref[...]
    return (jnp.exp(jnp.sum(lv[0:1] * lv[1:2], axis=-1, keepdims=True))
            - jnp.exp(jnp.sum(lv[2:3] * lv[3:4], axis=-1, keepdims=True)) + lam_init)


def _diff_out(o1, o2, lam, g, z, lam_init):
    o = o1 - lam * o2
    on = o * lax.rsqrt(jnp.mean(o * o, axis=-1, keepdims=True) + EPS) * g * (1.0 - lam_init)
    return on * _silu(z)


def _attn_prompt_kernel(rb_ref, q_ref, k_ref, vt_ref, z_ref, lam_ref, g_ref, o_ref,
                        m_ref, l_ref, acc_ref, bias_ref, *, tile, lam_init):
    t = tile
    h, i = pl.program_id(0), pl.program_id(1)

    @pl.when(i == 0)
    def _():
        blk = REL_MAX_DIST
        nb = t // blk
        si = lax.broadcasted_iota(jnp.int32, (blk, blk), 0)
        ti = lax.broadcasted_iota(jnp.int32, (blk, blk), 1)
        on_diag = jnp.where(si <= ti, _rel_bias_tile(ti - si, rb_ref, h), NEG)
        next_blk = _rel_bias_tile(ti - si + blk, rb_ref, h)
        zero = jnp.zeros((blk, blk), F32)
        masked = jnp.full((blk, blk), NEG, F32)
        for kb in range(nb):
            for qb in range(nb):
                rows, cols = slice(kb * blk, (kb + 1) * blk), slice(qb * blk, (qb + 1) * blk)
                bias_ref[0, rows, cols] = (on_diag if qb == kb else next_blk if qb == kb + 1
                                           else zero if qb > kb else masked)
                bias_ref[1, rows, cols] = next_blk if (kb == nb - 1 and qb == 0) else zero

    m_ref[...] = jnp.full(m_ref.shape, NEG, F32)
    l_ref[...] = jnp.zeros(l_ref.shape, F32)
    acc_ref[...] = jnp.zeros(acc_ref.shape, F32)

    def update(j, bias, krows=None, lanes=None):
        krows = slice(0, t) if krows is None else krows
        lanes = slice(0, t) if lanes is None else lanes
        k = k_ref[j][krows, :]
        vt = vt_ref[j][:, krows]
        sts = []
        for c in range(2):
            sl = slice(c * HEAD_DIM, (c + 1) * HEAD_DIM)
            st = lax.dot_general(k[:, sl], q_ref[lanes, sl], _NT, preferred_element_type=F32)
            sts.append(st if bias is None else st + bias)
        ps = []
        for c in range(2):
            m_prev = m_ref[c, :, lanes]
            m_new = jnp.maximum(m_prev, jnp.max(sts[c], axis=0, keepdims=True))
            alpha = jnp.exp2(m_prev - m_new)
            p = jnp.exp2(sts[c] - m_new)
            l_ref[c, :, lanes] = alpha * l_ref[c, :, lanes] + jnp.sum(p, axis=0, keepdims=True)
            m_ref[c, :, lanes] = m_new
            ps.append((alpha, p.astype(BF16)))
        for c in range(2):
            alpha, pb = ps[c]
            acc_ref[c, :, lanes] = alpha * acc_ref[c, :, lanes] + jnp.dot(vt, pb, preferred_element_type=F32)

    def update2(j0):
        kk = [k_ref[j0], k_ref[j0 + 1]]
        vts = [vt_ref[j0], vt_ref[j0 + 1]]
        sts = [[lax.dot_general(kk[u][:, c * HEAD_DIM:(c + 1) * HEAD_DIM],
                                q_ref[:, c * HEAD_DIM:(c + 1) * HEAD_DIM], _NT,
                                preferred_element_type=F32) for u in range(2)] for c in range(2)]
        ps = []
        for c in range(2):
            m_prev = m_ref[c]
            m_new = jnp.maximum(m_prev, jnp.maximum(jnp.max(sts[c][0], axis=0, keepdims=True),
                                                    jnp.max(sts[c][1], axis=0, keepdims=True)))
            alpha = jnp.exp2(m_prev - m_new)
            p0 = jnp.exp2(sts[c][0] - m_new)
            p1 = jnp.exp2(sts[c][1] - m_new)
            l_ref[c] = (alpha * l_ref[c] + jnp.sum(p0, axis=0, keepdims=True)
                        + jnp.sum(p1, axis=0, keepdims=True))
            m_ref[c] = m_new
            ps.append((alpha, p0.astype(BF16), p1.astype(BF16)))
        for c in range(2):
            alpha, p0, p1 = ps[c]
            acc_ref[c] = (alpha * acc_ref[c] + jnp.dot(vts[0], p0, preferred_element_type=F32)
                          + jnp.dot(vts[1], p1, preferred_element_type=F32))

    n_far = jnp.maximum(i - 1, 0)

    def far_pair(jj, carry):
        update2(2 * jj)
        return carry

    lax.fori_loop(0, n_far // 2, far_pair, 0)

    @pl.when(n_far % 2 == 1)
    def _():
        update(n_far - 1, None)

    @pl.when(i >= 1)
    def _():
        update(i - 1, bias_ref[1])

    half = t // 2
    update(i, bias_ref[0, :half, :], krows=slice(0, half))
    update(i, bias_ref[0, half:, half:], krows=slice(half, t), lanes=slice(half, t))
    lam = _lambda(lam_ref, lam_init)
    ot = acc_ref[0] / l_ref[0] - lam * (acc_ref[1] / l_ref[1])
    o = ot.T
    on = o * lax.rsqrt(jnp.mean(o * o, axis=-1, keepdims=True) + EPS) * g_ref[...] * (1.0 - lam_init)
    o_ref[...] = (on * _silu(z_ref[...])).astype(o_ref.dtype)


def _attn_prompt(aq, ak, avt, z, z_col, rel_bias, da_lambda, da_subln, *, seq, heads, lam_init):
    hw = 2 * HEAD_DIM
    nq, _, tile = avt.shape
    assert tile % REL_MAX_DIST == 0 and nq * tile == seq
    cz = z_col // hw
    return pl.pallas_call(
        functools.partial(_attn_prompt_kernel, tile=tile, lam_init=lam_init),
        grid=(heads, nq),
        in_specs=[pl.BlockSpec(memory_space=pltpu.SMEM),
                  pl.BlockSpec((tile, hw), lambda h, i: (i, h)),
                  pl.BlockSpec((nq, tile, hw), lambda h, i: (0, 0, h)),
                  pl.BlockSpec((nq, hw, tile), lambda h, i: (0, h, 0)),
                  pl.BlockSpec((tile, hw), lambda h, i: (i, cz + h)),
                  pl.BlockSpec((4, HEAD_DIM), lambda h, i: (0, 0)),
                  pl.BlockSpec((1, hw), lambda h, i: (0, 0))],
        out_specs=pl.BlockSpec((tile, hw), lambda h, i: (i, h)),
        out_shape=jax.ShapeDtypeStruct((seq, heads * hw), BF16),
        scratch_shapes=[pltpu.VMEM((2, 1, tile), F32), pltpu.VMEM((2, 1, tile), F32),
                        pltpu.VMEM((2, hw, tile), F32), pltpu.VMEM((2, tile, tile), F32)],
        compiler_params=_params("arbitrary", "arbitrary"),
        name="attn_prompt",
    )(rel_bias, aq, ak.reshape(nq, tile, heads * hw), avt, z, da_lambda, da_subln.reshape(1, hw))


def _attn_decode_kernel(pt_ref, rb_ref, q_ref, kn_ref, vn_ref, z_ref, kc_hbm, vc_hbm, lam_ref, g_ref,
                        o_ref, kbuf_ref, vbuf_ref, sem_ref, qx_ref, m_ref, l_ref, acc_ref, bias_ref,
                        nbias_ref, *, heads, page, n_pages, dec, group, n_buf, batch, lam_init):
    b, p = pl.program_id(0), pl.program_id(1)
    hw = 2 * HEAD_DIM
    r = heads * dec
    n = heads * page
    n_groups = n_pages // group
    total = batch * n_groups
    past = n_pages * page

    def group_copies(g):
        slot = g % n_buf
        seq, first = g // n_groups, (g % n_groups) * group
        copies = []
        for gi in range(group):
            pid = pt_ref[seq, first + gi]
            copies.append(pltpu.make_async_copy(kc_hbm.at[pid], kbuf_ref.at[slot, gi], sem_ref.at[slot]))
            copies.append(pltpu.make_async_copy(vc_hbm.at[pid], vbuf_ref.at[slot, gi], sem_ref.at[slot]))
        return copies

    def start_group(g):
        @pl.when(g < total)
        def _():
            for cp in group_copies(g):
                cp.start()

    def near_bias(rel):
        return jnp.concatenate([_rel_bias_tile(rel[h * dec:(h + 1) * dec], rb_ref, h)
                                for h in range(heads)], axis=0)

    @pl.when((b == 0) & (p == 0))
    def _():
        for g0 in range(n_buf - 1):
            start_group(jnp.int32(g0))
        row = lax.broadcasted_iota(jnp.int32, (r, n), 0)
        col = lax.broadcasted_iota(jnp.int32, (r, n), 1)
        valid = (col % heads) == (row // dec)
        bias_ref[0] = jnp.where(valid, 0.0, NEG)
        rel = (past + row % dec) - ((n_pages - 1) * page + col // heads)
        bias_ref[1] = jnp.where(valid, near_bias(rel), NEG)
        row = lax.broadcasted_iota(jnp.int32, (r, r), 0)
        col = lax.broadcasted_iota(jnp.int32, (r, r), 1)
        rel = row % dec - col % dec
        valid = jnp.where((col // dec) == (row // dec), rel, -1) >= 0
        nbias_ref[...] = jnp.where(valid, near_bias(rel), NEG)

    @pl.when(p == 0)
    def _():
        m_ref[...] = jnp.full(m_ref.shape, NEG, F32)
        l_ref[...] = jnp.zeros(l_ref.shape, F32)
        acc_ref[...] = jnp.zeros(acc_ref.shape, F32)
        for c in range(2):
            qx_ref[c] = jnp.concatenate(
                [q_ref[:, (2 * h + c) * HEAD_DIM:(2 * h + c + 1) * HEAD_DIM] for h in range(heads)],
                axis=0).astype(BF16)

    def attend(tiles):
        scores = [[lax.dot_general(qx_ref[c], keys_of(c).astype(BF16), _NT,
                                   preferred_element_type=F32) + bias for keys_of, _, bias in tiles]
                  for c in range(2)]
        ps, alphas = [], []
        for c in range(2):
            rows = slice(c * r, (c + 1) * r)
            m_prev = m_ref[rows]
            m_new = m_prev
            for s in scores[c]:
                m_new = jnp.maximum(m_new, jnp.max(s, axis=-1, keepdims=True))
            alpha = jnp.exp2(m_prev - m_new)
            prs = [jnp.exp2(s - m_new) for s in scores[c]]
            rowsum = jnp.sum(prs[0], axis=-1, keepdims=True)
            for pr in prs[1:]:
                rowsum = rowsum + jnp.sum(pr, axis=-1, keepdims=True)
            l_ref[rows] = alpha * l_ref[rows] + rowsum
            m_ref[rows] = m_new
            ps.append([pr.astype(BF16) for pr in prs])
            alphas.append(alpha)
        pv = None
        for ti, (_, vb, _) in enumerate(tiles):
            part = jnp.dot(jnp.concatenate([ps[0][ti], ps[1][ti]], axis=0), vb,
                           preferred_element_type=F32)
            pv = part if pv is None else pv + part
        acc_ref[...] = jnp.concatenate(alphas, axis=0) * acc_ref[...] + pv

    def page_tiles(slot, last_bias):
        tiles = []
        for gi in range(group):
            k_page = kbuf_ref.at[slot, gi]
            bias = bias_ref[last_bias if gi == group - 1 else 0]
            tiles.append((lambda c, k_page=k_page: k_page[pl.ds(c, n, stride=2), :],
                          vbuf_ref[slot, gi].astype(BF16), bias))
        return tiles

    @pl.when(p < n_groups)
    def _():
        g = b * n_groups + p
        start_group(g + n_buf - 1)
        for cp in group_copies(g):
            cp.wait()

    @pl.when(p < n_groups - 1)
    def _():
        attend(page_tiles((b * n_groups + p) % n_buf, 0))

    @pl.when(p == n_groups - 1)
    def _():
        attend(page_tiles((b * n_groups + p) % n_buf, 1))

    @pl.when(p == n_groups)
    def _():
        def new_keys(c):
            return jnp.concatenate(
                [kn_ref[:, (2 * h + c) * HEAD_DIM:(2 * h + c + 1) * HEAD_DIM] for h in range(heads)],
                axis=0)
        vnew = jnp.concatenate([vn_ref[:, h * hw:(h + 1) * hw] for h in range(heads)], axis=0)
        attend([(new_keys, vnew.astype(BF16), nbias_ref[...])])
        lam = _lambda(lam_ref, lam_init)
        acc = acc_ref[...] / l_ref[...]
        for h in range(heads):
            sl = slice(h * hw, (h + 1) * hw)
            o_ref[:, sl] = _diff_out(acc[h * dec:(h + 1) * dec], acc[r + h * dec:r + (h + 1) * dec],
                                     lam, g_ref[...], z_ref[:, sl], lam_init).astype(o_ref.dtype)


def _attn_decode(aq, ak, av, z, z_col, cache_k, cache_v, page_table, rel_bias, da_lambda, da_subln, *,
                 batch, dec, heads, lam_init):
    hw = 2 * HEAD_DIM
    w = heads * hw
    n_pool, page = cache_k.shape[0], cache_k.shape[1]
    n_pages = page_table.shape[1]
    group = math.gcd(n_pages, PAGES_PER_STEP)
    n_groups = n_pages // group
    assert page >= REL_MAX_DIST and dec <= REL_MAX_DIST
    kc = cache_k.reshape(n_pool, page * heads * 2, HEAD_DIM)
    vc = cache_v.reshape(n_pool, page * heads, hw)
    r = heads * dec

    def row_spec(cb):
        return pl.BlockSpec((dec, w), lambda b, p, pt: (b, cb))

    grid_spec = pltpu.PrefetchScalarGridSpec(
        num_scalar_prefetch=1,
        grid=(batch, n_groups + 1),
        in_specs=[pl.BlockSpec(memory_space=pltpu.SMEM),
                  row_spec(0), row_spec(0), row_spec(0), row_spec(z_col // w),
                  pl.BlockSpec(memory_space=pl.ANY), pl.BlockSpec(memory_space=pl.ANY),
                  pl.BlockSpec((4, HEAD_DIM), lambda b, p, pt: (0, 0)),
                  pl.BlockSpec((1, hw), lambda b, p, pt: (0, 0))],
        out_specs=pl.BlockSpec((dec, w), lambda b, p, pt: (b, 0)),
        scratch_shapes=[pltpu.VMEM((PAGE_RING, group, page * heads * 2, HEAD_DIM), F32),
                        pltpu.VMEM((PAGE_RING, group, page * heads, hw), F32),
                        pltpu.SemaphoreType.DMA((PAGE_RING,)),
                        pltpu.VMEM((2, r, HEAD_DIM), BF16),
                        pltpu.VMEM((2 * r, 1), F32), pltpu.VMEM((2 * r, 1), F32),
                        pltpu.VMEM((2 * r, hw), F32),
                        pltpu.VMEM((2, r, heads * page), F32), pltpu.VMEM((r, r), F32)],
    )
    return pl.pallas_call(
        functools.partial(_attn_decode_kernel, heads=heads, page=page, n_pages=n_pages, dec=dec,
                          group=group, n_buf=PAGE_RING, batch=batch, lam_init=lam_init),
        grid_spec=grid_spec,
        out_shape=jax.ShapeDtypeStruct((batch * dec, w), F32),
        compiler_params=_params("arbitrary", "arbitrary"),
        name="attn_decode",
    )(page_table, rel_bias, aq, ak, av, z, kc, vc, da_lambda, da_subln.reshape(1, hw))


def _up_kernel(oh_ref, oa_ref, wh_ref, wa_ref, gh_ref, ga_ref, o_ref, whb_ref, wab_ref):
    @pl.when(pl.program_id(1) == 0)
    def _():
        whb_ref[...] = wh_ref[...].astype(BF16)
        wab_ref[...] = wa_ref[...].astype(BF16)

    uh = jnp.dot(oh_ref[...].astype(BF16), whb_ref[...], preferred_element_type=F32)
    ua = jnp.dot(oa_ref[...].astype(BF16), wab_ref[...], preferred_element_type=F32)
    o_ref[...] = (jax.nn.sigmoid(gh_ref[...]) * uh + jax.nn.sigmoid(ga_ref[...]) * ua).astype(o_ref.dtype)


def _up_merge(o_h, o_a, w_h, w_a, gates, gh_col, ga_col):
    m, kh = o_h.shape
    ka = o_a.shape[1]
    d = w_h.shape[1]
    tm, tn = _tile(m, ROW_TILE), _tile(d, PROJ_COLS)
    cgh, cga = gh_col // tn, ga_col // tn
    return pl.pallas_call(
        _up_kernel,
        grid=(d // tn, m // tm),
        in_specs=[pl.BlockSpec((tm, kh), lambda j, i: (i, 0)),
                  pl.BlockSpec((tm, ka), lambda j, i: (i, 0)),
                  pl.BlockSpec((kh, tn), lambda j, i: (0, j)),
                  pl.BlockSpec((ka, tn), lambda j, i: (0, j)),
                  pl.BlockSpec((tm, tn), lambda j, i: (i, cgh + j)),
                  pl.BlockSpec((tm, tn), lambda j, i: (i, cga + j))],
        out_specs=pl.BlockSpec((tm, tn), lambda j, i: (i, j)),
        out_shape=jax.ShapeDtypeStruct((m, d), BF16),
        scratch_shapes=[pltpu.VMEM((kh, tn), BF16), pltpu.VMEM((ka, tn), BF16)],
        compiler_params=_params("arbitrary", "arbitrary"),
        name="up_merge",
    )(o_h, o_a, w_h, w_a, gates, gates)


def _out_kernel(a_ref, w_ref, x_ref, g_ref, y_ref):
    out = jnp.dot(a_ref[...], w_ref[...], preferred_element_type=F32)
    ms = jnp.mean(out * out, axis=-1, keepdims=True)
    y_ref[...] = x_ref[...] + out * lax.rsqrt(ms + EPS) * g_ref[...]


def _out_proj(merged, w_out, x, g_post):
    m, d = x.shape
    tm = _tile(m, OUT_ROWS)
    return pl.pallas_call(
        _out_kernel,
        grid=(m // tm,),
        in_specs=[pl.BlockSpec((tm, d), lambda i: (i, 0)),
                  pl.BlockSpec((d, d), lambda i: (0, 0), pipeline_mode=pl.Buffered(1)),
                  pl.BlockSpec((tm, d), lambda i: (i, 0)),
                  pl.BlockSpec((1, d), lambda i: (0, 0))],
        out_specs=pl.BlockSpec((tm, d), lambda i: (i, 0)),
        out_shape=jax.ShapeDtypeStruct((m, d), F32),
        compiler_params=_params("arbitrary"),
        name="out_proj",
    )(merged, w_out, x, g_post.reshape(1, d))


def _mix_and_project(x2, o_h, o_a, gates, w, da_w):
    d = x2.shape[1]
    merged = _up_merge(o_h, o_a, w["w_up_hg"], w["w_up_da"], gates, da_w, da_w + d)
    return _out_proj(merged, w["w_out"], x2, w["g_post"])


def _layer(xp, xs, s0s, cache_k, cache_v, page_table, w, *, layer, lam_init):
    _, seq, d = xp.shape
    bsz, dec, _ = xs.shape
    assert xp.shape[0] == 1
    hg_heads = (d // 2) // HEAD_DIM
    da_heads = (d // 2) // (2 * HEAD_DIM)
    hg_w = hg_heads * HEAD_DIM
    da_w = da_heads * 2 * HEAD_DIM
    col_aq = 4 * hg_w
    qscale = LOG2E * HEAD_DIM ** -0.5
    f32 = (F32, 1.0, "rows")

    xp2, xs2 = xp.reshape(seq, d), xs.reshape(bsz * dec, d)
    xnp, xns = _rmsnorm_cast(xp2, w["g_pre"]), _rmsnorm_cast(xs2, w["g_pre"])
    proj = functools.partial(_proj, xnp, w["w_in"], tn_pref=PROJ_COLS, name="in_proj", a2=xns)
    hg_p, hg_s = proj(0, 4 * hg_w, [f32], outs2=[f32])
    aq_p, aq_s = proj(col_aq, da_w, [(BF16, qscale, "rows")], outs2=[(F32, qscale, "rows")])
    k_p, k16_p, k_s = _kproj(xnp, xns, w["w_in"], col_aq + da_w, da_w)
    v_p, vt16_p, v_s = proj(col_aq + 2 * da_w, da_w, [f32, (BF16, 1.0, "tiles_t")], outs2=[f32])
    gates_p, gates_s = proj(col_aq + 3 * da_w, da_w + 2 * d, [f32], outs2=[f32])

    s0p = jnp.zeros((1, hg_heads, HEAD_DIM, HEAD_DIM), F32)
    hgrn = functools.partial(_hgrn, hg_lower=w["hg_lower"], hg_norm=w["hg_norm"], heads=hg_heads, layer=layer,
                             hb=math.gcd(hg_heads, HGRN_HEADS))
    oh_p, sp = hgrn(hg_p, s0=s0p, batch=1, seq=seq, chunk=math.gcd(seq, HGRN_CHUNK), out_dtype=BF16)
    oh_s, ss = hgrn(hg_s, s0=s0s, batch=bsz, seq=dec, chunk=math.gcd(dec, HGRN_DEC_CHUNK), out_dtype=F32)

    attn_w = (w["rel_bias"], w["da_lambda"], w["da_subln"])
    oa_p = _attn_prompt(aq_p, k16_p, vt16_p, gates_p, 0, *attn_w, seq=seq, heads=da_heads, lam_init=lam_init)
    oa_s = _attn_decode(aq_s, k_s, v_s, gates_s, 0, cache_k, cache_v, page_table, *attn_w, batch=bsz,
                        dec=dec, heads=da_heads, lam_init=lam_init)

    yp = _mix_and_project(xp2, oh_p, oa_p, gates_p, w, da_w).reshape(xp.shape)
    ys = _mix_and_project(xs2, oh_s, oa_s, gates_s, w, da_w).reshape(xs.shape)
    kv_p = (k_p.reshape(1, seq, da_heads, 2, HEAD_DIM), v_p.reshape(1, seq, da_heads, 2 * HEAD_DIM))
    kv_s = (k_s.reshape(bsz, dec, da_heads, 2, HEAD_DIM), v_s.reshape(bsz, dec, da_heads, 2 * HEAD_DIM))
    return yp, ys, kv_p + (sp,), kv_s + (ss,)


def kernel(x_prompt, x_sample, cache_k, cache_v, state_hgrn, page_table, norm_pre, norm_post, w_in,
           hg_lower, hg_norm, da_lambda, da_subln, rel_bias, w_up_hg, w_up_da, w_out):
    depth = w_in.shape[0]
    hp, hs = x_prompt, x_sample
    outs = [[] for _ in range(6)]
    for l in range(depth):
        lam_init = 0.8 - 0.6 * math.exp(-0.3 * l)
        w = dict(g_pre=norm_pre[l], g_post=norm_post[l], w_in=w_in[l], hg_lower=hg_lower,
                 hg_norm=hg_norm[l], da_lambda=da_lambda[l], da_subln=da_subln[l], rel_bias=rel_bias,
                 w_up_hg=w_up_hg[l], w_up_da=w_up_da[l], w_out=w_out[l].astype(BF16))
        hp, hs, new_p, new_s = _layer(hp, hs, state_hgrn[l], cache_k[l], cache_v[l], page_table, w,
                                      layer=l, lam_init=lam_init)
        for lst, val in zip(outs, new_p + new_s):
            lst.append(val)
    return (hp, hs) + tuple(jnp.stack(o) for o in outs)
```
